```python
import jax, jax.numpy as jnp
from jax import lax
import numpy as np

D_MODEL = 2048
BATCH = 4
SEQ = 2048
DEPTH = 4

GRID_W = 64
CTX_LEN = 256
D_MIX = D_MODEL
W_CONV = D_MIX // 2
W_MLSTM = D_MIX - W_CONV
CONV_K = 31
CONV_PAD = CONV_K // 2
H_ML = 8
DH_ML = W_MLSTM // H_ML
CHUNK = 64
N_GATES = 4 * H_ML
N_IN = 3 * W_CONV + 5 * W_MLSTM + N_GATES
EPS = 1e-6
NEG = -1e30

kernel_name = "hymba_conformer_mlstm_prefix_dit"


def rmsnorm(x, g):
    x32 = x.astype(jnp.float32)
    r = x32 * lax.rsqrt(jnp.mean(x32 * x32, axis=-1, keepdims=True) + EPS)
    return (r * g.astype(jnp.float32)).astype(x.dtype)


def layernorm(x, g, b):
    x32 = x.astype(jnp.float32)
    mu = jnp.mean(x32, axis=-1, keepdims=True)
    xc = x32 - mu
    r = xc * lax.rsqrt(jnp.mean(xc * xc, axis=-1, keepdims=True) + EPS)
    return (r * g.astype(jnp.float32) + b.astype(jnp.float32)).astype(x.dtype)


def conv_latent(u, w, b):
    bsz, n, ch = u.shape
    rows = n // GRID_W
    grid = u.reshape(bsz, rows, GRID_W, ch)
    half = ch // 2
    dn = ('NHWC', 'HWIO', 'NHWC')
    yh = lax.conv_general_dilated(grid[..., :half], w[None, :, None, :half], (1, 1),
                                  [(0, 0), (CONV_PAD, CONV_PAD)], dimension_numbers=dn,
                                  feature_group_count=half)
    yv = lax.conv_general_dilated(grid[..., half:], w[:, None, None, half:], (1, 1),
                                  [(CONV_PAD, CONV_PAD), (0, 0)], dimension_numbers=dn,
                                  feature_group_count=ch - half)
    return jnp.concatenate([yh, yv], axis=-1).reshape(bsz, n, ch) + b


def conv_context(u, w, b):
    ch = u.shape[-1]
    y = lax.conv_general_dilated(u, w[:, None, :], (1,), [(CONV_PAD, CONV_PAD)],
                                 dimension_numbers=('NWC', 'WIO', 'NWC'),
                                 feature_group_count=ch)
    return y + b


def conformer_branch(a, g, z, conv_fn, w_dw, b_dw, ln_g, ln_b, w_pw2):
    u = a * jax.nn.sigmoid(g)
    u = conv_fn(u, w_dw, b_dw)
    u = layernorm(u, ln_g, ln_b)
    u = jax.nn.silu(u) @ w_pw2
    return u * jax.nn.silu(z)


def to_heads(p):
    bsz, n, _ = p.shape
    return p.reshape(bsz, n, H_ML, DH_ML).transpose(0, 2, 1, 3).astype(jnp.float32)


def mlstm_scan(q, k, v, log_i, log_f, state):
    bsz, nh, n, dh = q.shape
    nc = n // CHUNK

    def chunks(t):
        return jnp.moveaxis(t.reshape(t.shape[:2] + (nc, CHUNK) + t.shape[3:]), 2, 0)

    tril = jnp.tril(jnp.ones((CHUNK, CHUNK), dtype=bool))

    def step(carry, xs):
        c_st, n_st, m_st = carry
        qc, kc, vc, li, lf = xs
        b = jnp.cumsum(lf, axis=-1)
        dmat = b[..., :, None] - b[..., None, :] + li[..., None, :]
        dmat = jnp.where(tril, dmat, NEG)
        inter = b + m_st[..., None]
        m_t = jnp.maximum(inter, jnp.max(dmat, axis=-1))
        w_inter = jnp.exp(inter - m_t)
        pmat = jnp.exp(dmat - m_t[..., None]) * jnp.einsum('bhtd,bhsd->bhts', qc, kc)
        num = (w_inter[..., None] * jnp.einsum('bhtd,bhde->bhte', qc, c_st)
               + jnp.einsum('bhts,bhse->bhte', pmat, vc))
        den = w_inter * jnp.einsum('bhtd,bhd->bht', qc, n_st) + jnp.sum(pmat, axis=-1)
        h = num / jnp.maximum(jnp.abs(den), jnp.exp(-m_t))[..., None]
        b_last = b[..., -1]
        gl = b_last[..., None] - b + li
        m_new = jnp.maximum(b_last + m_st, jnp.max(gl, axis=-1))
        a = jnp.exp(b_last + m_st - m_new)
        ws = jnp.exp(gl - m_new[..., None])
        c_new = a[..., None, None] * c_st + jnp.einsum('bhsd,bhse->bhde', kc * ws[..., None], vc)
        n_new = a[..., None] * n_st + jnp.einsum('bhs,bhsd->bhd', ws, kc)
        return (c_new, n_new, m_new), h

    state, hs = lax.scan(step, state, (chunks(q), chunks(k), chunks(v), chunks(log_i), chunks(log_f)))
    h = jnp.moveaxis(hs, 0, 2).reshape(bsz, nh, n, dh)
    return h, state


def split_gates(gp, b_gate):
    g = (gp.astype(jnp.float32) + b_gate.astype(jnp.float32)).transpose(0, 2, 1)
    i_f, f_f, i_b, f_b = jnp.split(g, 4, axis=1)
    return i_f, jax.nn.log_sigmoid(f_f), i_b, jax.nn.log_sigmoid(f_b)


def mlstm_bidir(qx, kx, vx, gx, qc, kc, vc, gc, b_gate):
    scale = DH_ML ** -0.5
    qx, kx, vx = to_heads(qx) * scale, to_heads(kx), to_heads(vx)
    qc, kc, vc = to_heads(qc) * scale, to_heads(kc), to_heads(vc)
    ixf, fxf, ixb, fxb = split_gates(gx, b_gate)
    icf, fcf, icb, fcb = split_gates(gc, b_gate)
    bsz = qx.shape[0]
    init = (jnp.zeros((bsz, H_ML, DH_ML, DH_ML), jnp.float32),
            jnp.zeros((bsz, H_ML, DH_ML), jnp.float32),
            jnp.full((bsz, H_ML), NEG, jnp.float32))
    flip = lambda t: jnp.flip(t, axis=2)
    hcf, st_f = mlstm_scan(qc, kc, vc, icf, fcf, init)
    hxf, _ = mlstm_scan(qx, kx, vx, ixf, fxf, st_f)
    hcb, st_b = mlstm_scan(flip(qc), flip(kc), flip(vc), flip(icb), flip(fcb), init)
    hxb, _ = mlstm_scan(flip(qx), flip(kx), flip(vx), flip(ixb), flip(fxb), st_b)
    return hxf + flip(hxb), hcf + flip(hcb)


def mlstm_out(h, o, z, g_head, dtype):
    mu = jnp.mean(h, axis=-1, keepdims=True)
    hc = h - mu
    hn = hc * lax.rsqrt(jnp.mean(hc * hc, axis=-1, keepdims=True) + EPS)
    bsz, _, n, _ = h.shape
    hn = hn.transpose(0, 2, 1, 3).reshape(bsz, n, W_MLSTM) * g_head.astype(jnp.float32)
    return (hn * jax.nn.sigmoid(o.astype(jnp.float32))).astype(dtype) * jax.nn.silu(z)


def split_proj(p):
    cuts = np.cumsum([W_CONV, W_CONV, W_CONV, W_MLSTM, W_MLSTM, W_MLSTM, W_MLSTM, W_MLSTM])
    return jnp.split(p, [int(t) for t in cuts], axis=-1)


def setup_inputs(seed: int = 0) -> dict:
    key = jax.random.key(seed)
    ks = jax.random.split(key, 17)
    nrm = jax.random.normal
    f32 = jnp.float32
    gate_base = jnp.concatenate([jnp.zeros((H_ML,), f32), jnp.full((H_ML,), 3.0, f32),
                                 jnp.zeros((H_ML,), f32), jnp.full((H_ML,), 3.0, f32)])
    return {
        "x": nrm(ks[0], (BATCH, SEQ, D_MODEL), f32),
        "c": nrm(ks[1], (BATCH, D_MODEL), f32),
        "ctx": nrm(ks[2], (BATCH, CTX_LEN, D_MODEL), f32),
        "c_ctx": nrm(ks[3], (D_MODEL,), f32),
        "w_ada": nrm(ks[4], (DEPTH, D_MODEL, 3 * D_MODEL), f32) * (0.5 * D_MODEL ** -0.5),
        "b_ada": nrm(ks[5], (DEPTH, 3 * D_MODEL), f32) * 0.01,
        "g_pre": 1.0 + 0.1 * nrm(ks[6], (DEPTH, D_MODEL), f32),
        "g_post": 1.0 + 0.1 * nrm(ks[7], (DEPTH, D_MODEL), f32),
        "w_in": nrm(ks[8], (DEPTH, D_MODEL, N_IN), f32) * D_MODEL ** -0.5,
        "b_gate": gate_base + 0.3 * nrm(ks[9], (DEPTH, N_GATES), f32),
        "w_dw": nrm(ks[10], (DEPTH, CONV_K, W_CONV), f32) * CONV_K ** -0.5,
        "b_dw": nrm(ks[11], (DEPTH, W_CONV), f32) * 0.01,
        "ln_g": 1.0 + 0.1 * nrm(ks[12], (DEPTH, W_CONV), f32),
        "ln_b": nrm(ks[13], (DEPTH, W_CONV), f32) * 0.01,
        "w_pw2": nrm(ks[14], (DEPTH, W_CONV, W_CONV), f32) * W_CONV ** -0.5,
        "g_head": 1.0 + 0.1 * nrm(ks[15], (DEPTH, W_MLSTM), f32),
        "w_out": nrm(ks[16], (DEPTH, D_MIX, D_MODEL), f32) * D_MIX ** -0.5,
    }


def reference(x, c, ctx, c_ctx, w_ada, b_ada, g_pre, g_post, w_in, b_gate, w_dw, b_dw,
              ln_g, ln_b, w_pw2, g_head, w_out):
    for l in range(DEPTH):
        last = l == DEPTH - 1
        ada_x = jax.nn.silu(c) @ w_ada[l] + b_ada[l]
        ada_c = jax.nn.silu(c_ctx) @ w_ada[l] + b_ada[l]
        sh_x, sc_x, gt_x = jnp.split(ada_x[:, None, :], 3, axis=-1)
        sh_c, sc_c, gt_c = jnp.split(ada_c, 3, axis=-1)
        hx = rmsnorm(x, g_pre[l]) * (1.0 + sc_x) + sh_x
        hc = rmsnorm(ctx, g_pre[l]) * (1.0 + sc_c) + sh_c
        ax, gx_, zx, qx, kx, vx, ox, zmx, gatex = split_proj(hx @ w_in[l])
        ac, gc_, zc, qc, kc, vc, oc, zmc, gatec = split_proj(hc @ w_in[l])
        h_x, h_c = mlstm_bidir(qx, kx, vx, gatex, qc, kc, vc, gatec, b_gate[l])
        ym_x = mlstm_out(h_x, ox, zmx, g_head[l], x.dtype)
        yc_x = conformer_branch(ax, gx_, zx, conv_latent, w_dw[l], b_dw[l], ln_g[l], ln_b[l], w_pw2[l])
        out_x = jnp.concatenate([yc_x, ym_x], axis=-1) @ w_out[l]
        x_new = x + gt_x * rmsnorm(out_x, g_post[l])
        if not last:
            ym_c = mlstm_out(h_c, oc, zmc, g_head[l], ctx.dtype)
            yc_c = conformer_branch(ac, gc_, zc, conv_context, w_dw[l], b_dw[l], ln_g[l], ln_b[l], w_pw2[l])
            out_c = jnp.concatenate([yc_c, ym_c], axis=-1) @ w_out[l]
            ctx = ctx + gt_c * rmsnorm(out_c, g_post[l])
        x = x_new
    return x
```

```python
import functools

import jax
import jax.numpy as jnp
from jax import lax
from jax.experimental import pallas as pl
from jax.experimental.pallas import tpu as pltpu

F32 = jnp.float32
BF16 = jnp.bfloat16

GRID_W = 64
CONV_K = 31
CONV_PAD = CONV_K // 2
EPS = 1e-6
NEG = -1e30
LANES = 128
CHUNK = 128
SEG_PAD = 16
VMEM_LIMIT = 56 * 1024 * 1024


def _cparams(sem):
    return pltpu.CompilerParams(dimension_semantics=sem, vmem_limit_bytes=VMEM_LIMIT)


def _sigmoid(x):
    return 1.0 / (1.0 + jnp.exp(-x))


def _ada_kernel(c_ref, w_ref, b_ref, o_ref):
    c = c_ref[...]
    s = (c * _sigmoid(c)).astype(BF16)
    o_ref[0] = jnp.dot(s, w_ref[0].astype(BF16), preferred_element_type=F32) + b_ref[0]


def _ada_all_layers(cond, w_ada, b_ada):
    depth, d, n = w_ada.shape
    tn = 1024 if n % 1024 == 0 else n // 3
    return pl.pallas_call(
        _ada_kernel,
        out_shape=jax.ShapeDtypeStruct((depth, 8, n), F32),
        grid=(depth, n // tn),
        in_specs=[pl.BlockSpec((8, d), lambda l, j: (0, 0)),
                  pl.BlockSpec((1, d, tn), lambda l, j: (l, 0, j)),
                  pl.BlockSpec((1, 1, tn), lambda l, j: (l, 0, j))],
        out_specs=pl.BlockSpec((1, 8, tn), lambda l, j: (l, 0, j)),
        compiler_params=_cparams(("arbitrary", "arbitrary")),
        name="ada",
    )(cond, w_ada, b_ada.reshape(depth, 1, n))


def _in_kernel(x_ref, sh_ref, sc_ref, gpre_ref, w_ref, wg_ref, p_ref, g_ref, hx_ref, *,
               tn, q_lo, q_hi, q_scale, sub):
    j = pl.program_id(1)

    @pl.when(j == 0)
    def _():
        tm = x_ref.shape[0]
        for r in range(0, tm, sub):
            x = x_ref[r:r + sub, :]
            ms = jnp.mean(x * x, axis=-1, keepdims=True)
            h = (x * lax.rsqrt(ms + EPS) * gpre_ref[...]) * (1.0 + sc_ref[0]) + sh_ref[0]
            hx_ref[r:r + sub, :] = h.astype(BF16)
        g_ref[...] = jnp.dot(hx_ref[...], wg_ref[...], preferred_element_type=F32)

    acc = jnp.dot(hx_ref[...], w_ref[...], preferred_element_type=F32)
    col = j * tn
    scale = jnp.where((col >= q_lo) & (col < q_hi), q_scale, 1.0).astype(F32)
    p_ref[...] = (acc * scale).astype(BF16)


def _in_proj(x2d, ada3, row_of_tile, g_pre, w_main, w_gate, *, tm, tn, w_conv, w_ml):
    m, d = x2d.shape
    n = w_main.shape[1]
    q_lo = 3 * w_conv
    kern = functools.partial(_in_kernel, tn=tn, q_lo=q_lo, q_hi=q_lo + w_ml,
                             q_scale=float(LANES) ** -0.5, sub=min(tm, 256))
    return pl.pallas_call(
        kern,
        out_shape=(jax.ShapeDtypeStruct((m, n), BF16), jax.ShapeDtypeStruct((m, LANES), F32)),
        grid=(m // tm, n // tn),
        in_specs=[pl.BlockSpec((tm, d), lambda i, j: (i, 0)),
                  pl.BlockSpec((1, 1, d), lambda i, j: (row_of_tile(i) * 3 + 0, 0, 0)),
                  pl.BlockSpec((1, 1, d), lambda i, j: (row_of_tile(i) * 3 + 1, 0, 0)),
                  pl.BlockSpec((1, d), lambda i, j: (0, 0)),
                  pl.BlockSpec((d, tn), lambda i, j: (0, j)),
                  pl.BlockSpec((d, LANES), lambda i, j: (0, 0))],
        out_specs=(pl.BlockSpec((tm, tn), lambda i, j: (i, j)),
                   pl.BlockSpec((tm, LANES), lambda i, j: (i, 0))),
        scratch_shapes=[pltpu.VMEM((tm, d), BF16)],
        compiler_params=_cparams(("arbitrary", "arbitrary")),
        name="in_proj",
    )(x2d, ada3, ada3, g_pre, w_main, w_gate)


def _gate_kernel(g_ref, bias_ref, bt_ref, rt_ref, *, nh):
    row = lax.broadcasted_iota(jnp.int32, (CHUNK, CHUNK), 0)
    col = lax.broadcasted_iota(jnp.int32, (CHUNK, CHUNK), 1)
    tril = jnp.where(row >= col, 1.0, 0.0).astype(BF16)
    triu = jnp.where(row <= col, 1.0, 0.0).astype(BF16)
    is_ff = (col >= nh) & (col < 2 * nh)
    is_fb = (col >= 3 * nh) & (col < 4 * nh)

    def csum(tri, parts):
        out = jnp.dot(tri, parts[0], preferred_element_type=F32)
        for p in parts[1:]:
            out = out + jnp.dot(tri, p, preferred_element_type=F32)
        return out

    for c in range(g_ref.shape[0] // CHUNK):
        g = g_ref[c * CHUNK:(c + 1) * CHUNK, :] + bias_ref[...]
        lf = jnp.minimum(g, 0.0) - jnp.log(1.0 + jnp.exp(-jnp.abs(g)))
        hi = lf.astype(BF16)
        r1 = lf - hi.astype(F32)
        mid = r1.astype(BF16)
        lo = (r1 - mid.astype(F32)).astype(BF16)
        parts = (hi, mid, lo)
        bmat = jnp.where(is_ff, csum(tril, parts), jnp.where(is_fb, csum(triu, parts), 0.0))
        li = pltpu.roll(g, nh, axis=1)
        rmat = li - bmat
        bt_ref[c] = bmat.T[0:4 * nh, :]
        rt_ref[c] = rmat.T[0:4 * nh, :]


def _gate_prep(g2d, bias, nh, rows):
    m = g2d.shape[0]
    nc = rows // CHUNK
    out = jax.ShapeDtypeStruct((m // CHUNK, 4 * nh, CHUNK), F32)
    return pl.pallas_call(
        functools.partial(_gate_kernel, nh=nh),
        out_shape=(out, out),
        grid=(m // rows,),
        in_specs=[pl.BlockSpec((rows, LANES), lambda i: (i, 0)),
                  pl.BlockSpec((1, LANES), lambda i: (0, 0))],
        out_specs=(pl.BlockSpec((nc, 4 * nh, CHUNK), lambda i: (i, 0, 0)),
                   pl.BlockSpec((nc, 4 * nh, CHUNK), lambda i: (i, 0, 0))),
        compiler_params=_cparams(("arbitrary",)),
        name="gate_prep",
    )(g2d, bias)


def _mlstm_kernel(qx_ref, kx_ref, vx_ref, ox_ref, zx_ref, qc_ref, kc_ref, vc_ref, oc_ref, zc_ref,
                  btx_ref, rtx_ref, btc_ref, rtc_ref, gh_ref, yx_ref, yc_ref,
                  kt_ref, hf_ref, hb_ref, st_ref, *, nh):
    h_idx = pl.program_id(1)
    row = lax.broadcasted_iota(jnp.int32, (CHUNK, CHUNK), 0)
    col = lax.broadcasted_iota(jnp.int32, (CHUNK, CHUNK), 1)
    mask_f = row >= col
    mask_b = row <= col
    ones_blk = jnp.where(col == 0, 1.0, 0.0).astype(BF16)
    ncx = qx_ref.shape[0] // CHUNK
    ncc = qc_ref.shape[0] // CHUNK

    st_ref[...] = jnp.zeros_like(st_ref)

    def step(c, base, fwd, m_prev, q_ref, v_ref, bt_ref, rt_ref, h_ref):
        r0 = pl.multiple_of(c * CHUNK, CHUNK)
        g_row = (nh if fwd else 3 * nh) + h_idx
        d = 0 if fwd else 1
        q = q_ref[pl.ds(r0, CHUNK), :]
        vaug = jnp.concatenate([v_ref[pl.ds(r0, CHUNK), :], ones_blk], axis=1)
        kt = kt_ref[:, pl.ds(pl.multiple_of(base + r0, CHUNK), CHUNK)]
        b_row = bt_ref[c, pl.ds(g_row, 1), :]
        r_row = rt_ref[c, pl.ds(g_row, 1), :]
        b_bc = jnp.broadcast_to(b_row, (CHUNK, CHUNK)).T
        b_col = b_bc[:, 0:1]
        b_last = b_row[:, CHUNK - 1:CHUNK] if fwd else b_row[:, 0:1]
        s = jnp.dot(q, kt.astype(BF16), preferred_element_type=F32)
        dm = jnp.where(mask_f if fwd else mask_b, b_bc + r_row, NEG)
        inter = b_col + m_prev
        m_t = jnp.maximum(inter, jnp.max(dm, axis=1, keepdims=True))
        w_inter = jnp.exp(inter - m_t)
        p = jnp.exp(dm - m_t) * s
        c_st = st_ref[d]
        num_aug = (w_inter * jnp.dot(q, c_st.astype(BF16), preferred_element_type=F32)
                   + jnp.dot(p.astype(BF16), vaug, preferred_element_type=F32))
        den = num_aug[:, LANES:LANES + 1]
        h_ref[pl.ds(pl.multiple_of(base + r0, CHUNK), CHUNK), :] = (
            num_aug[:, 0:LANES] / jnp.maximum(jnp.abs(den), jnp.exp(-m_t)))
        gl = b_last + r_row
        m_new = jnp.maximum(b_last + m_prev, jnp.max(gl, axis=1, keepdims=True))
        a = jnp.exp(b_last + m_prev - m_new)
        kws = (kt * jnp.exp(gl - m_new)).astype(BF16)
        st_ref[d] = a * c_st + jnp.dot(kws, vaug, preferred_element_type=F32)
        return m_new

    def segment(nc, base, carry, q_ref, k_ref, v_ref, bt_ref, rt_ref):
        def tr(c, _):
            r0 = pl.multiple_of(c * CHUNK, CHUNK)
            kt_ref[:, pl.ds(pl.multiple_of(base + r0, CHUNK), CHUNK)] = (
                k_ref[pl.ds(r0, CHUNK), :].astype(F32).T)
            return 0
        lax.fori_loop(0, nc, tr, 0)

        def body(i, ms):
            mf = step(i, base, True, ms[0], q_ref, v_ref, bt_ref, rt_ref, hf_ref)
            mb = step(nc - 1 - i, base, False, ms[1], q_ref, v_ref, bt_ref, rt_ref, hb_ref)
            return (mf, mb)
        return lax.fori_loop(0, nc, body, carry)

    m0 = jnp.full((1, 1), NEG, F32)
    carry = segment(ncc, 0, (m0, m0), qc_ref, kc_ref, vc_ref, btc_ref, rtc_ref)
    segment(ncx, ncc * CHUNK, carry, qx_ref, kx_ref, vx_ref, btx_ref, rtx_ref)

    def finish(nc, base, o_ref, z_ref, y_ref):
        def body(c, _):
            r0 = pl.multiple_of(c * CHUNK, CHUNK)
            rs = pl.multiple_of(base + r0, CHUNK)
            h = hf_ref[pl.ds(rs, CHUNK), :] + hb_ref[pl.ds(rs, CHUNK), :]
            mu = jnp.mean(h, axis=1, keepdims=True)
            hc = h - mu
            hn = hc * lax.rsqrt(jnp.mean(hc * hc, axis=1, keepdims=True) + EPS) * gh_ref[...]
            z = z_ref[pl.ds(r0, CHUNK), :].astype(F32)
            y = hn * _sigmoid(o_ref[pl.ds(r0, CHUNK), :].astype(F32)) * (z * _sigmoid(z))
            y_ref[pl.ds(r0, CHUNK), :] = y.astype(BF16)
            return 0
        lax.fori_loop(0, nc, body, 0)

    finish(ncc, 0, oc_ref, zc_ref, yc_ref)
    finish(ncx, ncc * CHUNK, ox_ref, zx_ref, yx_ref)


def _mlstm(px, pc, btx, rtx, btc, rtc, g_head, *, bsz, seq, ctx_len, nh, w_conv, w_ml):
    dh = LANES
    cb = 3 * w_conv // dh
    hb = w_ml // dh

    def colspec(rows, k):
        return pl.BlockSpec((rows, dh), lambda b, h: (b, cb + k * hb + h))

    def gspec(rows):
        return pl.BlockSpec((rows // CHUNK, 4 * nh, CHUNK), lambda b, h: (b, 0, 0))

    tot = seq + ctx_len
    return pl.pallas_call(
        functools.partial(_mlstm_kernel, nh=nh),
        out_shape=(jax.ShapeDtypeStruct((bsz * seq, w_ml), BF16),
                   jax.ShapeDtypeStruct((bsz * ctx_len, w_ml), BF16)),
        grid=(bsz, nh),
        in_specs=[colspec(seq, 0), colspec(seq, 1), colspec(seq, 2), colspec(seq, 3), colspec(seq, 4),
                  colspec(ctx_len, 0), colspec(ctx_len, 1), colspec(ctx_len, 2), colspec(ctx_len, 3),
                  colspec(ctx_len, 4),
                  gspec(seq), gspec(seq), gspec(ctx_len), gspec(ctx_len),
                  pl.BlockSpec((1, dh), lambda b, h: (0, h))],
        out_specs=(pl.BlockSpec((seq, dh), lambda b, h: (b, h)),
                   pl.BlockSpec((ctx_len, dh), lambda b, h: (b, h))),
        scratch_shapes=[pltpu.VMEM((dh, tot), F32),
                        pltpu.VMEM((tot, dh), F32),
                        pltpu.VMEM((tot, dh), F32),
                        pltpu.VMEM((2, dh, 2 * dh), F32)],
        compiler_params=_cparams(("arbitrary", "arbitrary")),
        name="mlstm",
    )(px, px, px, px, px, pc, pc, pc, pc, pc, btx, rtx, btc, rtc, g_head)


def _glu(a_ref, g_ref, r0, rows):
    a = a_ref[r0:r0 + rows, :].astype(F32)
    return a * _sigmoid(g_ref[r0:r0 + rows, :].astype(F32))


def _segconv_kernel(a_ref, g_ref, w_ref, b_ref, o_ref, pad_ref, *, seg, sub):
    rows, cw = a_ref.shape
    stride = seg + 2 * SEG_PAD
    zeros = jnp.zeros((SEG_PAD, cw), F32)
    for s in range(rows // seg):
        pad_ref[s * stride:s * stride + SEG_PAD, :] = zeros
        pad_ref[s * stride + SEG_PAD:s * stride + SEG_PAD + seg, :] = _glu(a_ref, g_ref, s * seg, seg)
        pad_ref[s * stride + SEG_PAD + seg:(s + 1) * stride, :] = zeros
    for s in range(rows // seg):
        for t0 in range(0, seg, sub):
            start = s * stride + SEG_PAD + t0 - CONV_PAD
            acc = jnp.zeros((sub, cw), F32) + b_ref[...]
            for k in range(CONV_K):
                acc = acc + pad_ref[start + k:start + k + sub, :] * w_ref[k:k + 1, :]
            o_ref[s * seg + t0:s * seg + t0 + sub, :] = acc


def _segconv(p, w_dw, b_dw, *, seg, rows, cw, ncol, a_col0, g_col0, w_col0):
    m = p.shape[0]
    stride = seg + 2 * SEG_PAD
    return pl.pallas_call(
        functools.partial(_segconv_kernel, seg=seg, sub=min(seg, 64)),
        out_shape=jax.ShapeDtypeStruct((m, ncol * cw), F32),
        grid=(m // rows, ncol),
        in_specs=[pl.BlockSpec((rows, cw), lambda i, j: (i, a_col0 + j)),
                  pl.BlockSpec((rows, cw), lambda i, j: (i, g_col0 + j)),
                  pl.BlockSpec((CONV_K, cw), lambda i, j: (0, w_col0 + j)),
                  pl.BlockSpec((1, cw), lambda i, j: (0, w_col0 + j))],
        out_specs=pl.BlockSpec((rows, cw), lambda i, j: (i, j)),
        scratch_shapes=[pltpu.VMEM((rows // seg * stride, cw), F32)],
        compiler_params=_cparams(("arbitrary", "arbitrary")),
        name="segconv",
    )(p, p, w_dw, b_dw)


def _rowconv_kernel(a_ref, g_ref, w_ref, b_ref, o_ref, pad_ref, *, nrows):
    cw = a_ref.shape[1]
    edge = CONV_PAD * GRID_W
    pad_ref[0:edge, :] = jnp.zeros((edge, cw), F32)
    pad_ref[edge + nrows * GRID_W:2 * edge + nrows * GRID_W, :] = jnp.zeros((edge, cw), F32)
    for r in range(nrows):
        pad_ref[edge + r * GRID_W:edge + (r + 1) * GRID_W, :] = _glu(a_ref, g_ref, r * GRID_W, GRID_W)

    def body(r, _):
        acc = jnp.zeros((GRID_W, cw), F32) + b_ref[...]
        for k in range(CONV_K):
            src = pl.multiple_of((r + k) * GRID_W, GRID_W)
            acc = acc + pad_ref[pl.ds(src, GRID_W), :] * w_ref[k:k + 1, :]
        o_ref[pl.ds(pl.multiple_of(r * GRID_W, GRID_W), GRID_W), :] = acc
        return 0
    lax.fori_loop(0, nrows, body, 0)


def _rowconv(p, w_dw, b_dw, *, seq, cw, ncol, a_col0, g_col0, w_col0):
    m = p.shape[0]
    nrows = seq // GRID_W
    return pl.pallas_call(
        functools.partial(_rowconv_kernel, nrows=nrows),
        out_shape=jax.ShapeDtypeStruct((m, ncol * cw), F32),
        grid=(m // seq, ncol),
        in_specs=[pl.BlockSpec((seq, cw), lambda i, j: (i, a_col0 + j)),
                  pl.BlockSpec((seq, cw), lambda i, j: (i, g_col0 + j)),
                  pl.BlockSpec((CONV_K, cw), lambda i, j: (0, w_col0 + j)),
                  pl.BlockSpec((1, cw), lambda i, j: (0, w_col0 + j))],
        out_specs=pl.BlockSpec((seq, cw), lambda i, j: (i, j)),
        scratch_shapes=[pltpu.VMEM((seq + 2 * CONV_PAD * GRID_W, cw), F32)],
        compiler_params=_cparams(("arbitrary", "arbitrary")),
        name="rowconv",
    )(p, p, w_dw, b_dw)


def _out_kernel(cva_ref, cvb_ref, z_ref, ym_ref, x_ref, gt_ref, lng_ref, lnb_ref, gpost_ref,
                wpw_ref, wout_ref, o_ref):
    w_conv = wpw_ref.shape[0]
    u = jnp.concatenate([cva_ref[...], cvb_ref[...]], axis=1)
    mu = jnp.mean(u, axis=-1, keepdims=True)
    uc = u - mu
    r = uc * lax.rsqrt(jnp.mean(uc * uc, axis=-1, keepdims=True) + EPS) * lng_ref[...] + lnb_ref[...]
    t = jnp.dot((r * _sigmoid(r)).astype(BF16), wpw_ref[...], preferred_element_type=F32)
    z = z_ref[...].astype(F32)
    yc = (t * (z * _sigmoid(z))).astype(BF16)
    out = (jnp.dot(yc, wout_ref[0:w_conv, :], preferred_element_type=F32)
           + jnp.dot(ym_ref[...], wout_ref[w_conv:, :], preferred_element_type=F32))
    ms = jnp.mean(out * out, axis=-1, keepdims=True)
    o_ref[...] = x_ref[...] + gt_ref[0] * (out * lax.rsqrt(ms + EPS) * gpost_ref[...])


def _out_proj(cva, cvb, cvb_col, p, ym, x2d, ada3, row_of_tile, ln_g, ln_b, g_post, w_pw2, w_out,
              *, tm, w_conv):
    m, d = x2d.shape
    half = w_conv // 2
    w_mix = w_out.shape[0]
    return pl.pallas_call(
        _out_kernel,
        out_shape=jax.ShapeDtypeStruct((m, d), F32),
        grid=(m // tm,),
        in_specs=[pl.BlockSpec((tm, half), lambda i: (i, 0)),
                  pl.BlockSpec((tm, half), lambda i: (i, cvb_col)),
                  pl.BlockSpec((tm, w_conv), lambda i: (i, 2)),
                  pl.BlockSpec((tm, w_mix - w_conv), lambda i: (i, 0)),
                  pl.BlockSpec((tm, d), lambda i: (i, 0)),
                  pl.BlockSpec((1, 1, d), lambda i: (row_of_tile(i) * 3 + 2, 0, 0)),
                  pl.BlockSpec((1, w_conv), lambda i: (0, 0)),
                  pl.BlockSpec((1, w_conv), lambda i: (0, 0)),
                  pl.BlockSpec((1, d), lambda i: (0, 0)),
                  pl.BlockSpec((w_conv, w_conv), lambda i: (0, 0)),
                  pl.BlockSpec((w_mix, d), lambda i: (0, 0))],
        out_specs=pl.BlockSpec((tm, d), lambda i: (i, 0)),
        compiler_params=_cparams(("arbitrary",)),
        name="out_proj",
    )(cva, cvb, p, ym, x2d, ada3, ln_g, ln_b, g_post, w_pw2, w_out)


def kernel(x, c, ctx, c_ctx, w_ada, b_ada, g_pre, g_post, w_in, b_gate, w_dw, b_dw, ln_g, ln_b, w_pw2,
           g_head, w_out):
    bsz, seq, d = x.shape
    ctx_len = ctx.shape[1]
    depth = w_ada.shape[0]
    w_conv = w_dw.shape[-1]
    w_ml = g_head.shape[-1]
    nh = b_gate.shape[-1] // 4
    n_main = 3 * w_conv + 5 * w_ml
    half = w_conv // 2
    assert w_ml == nh * LANES and 4 * nh <= LANES and w_conv == w_ml
    assert seq % CHUNK == 0 and ctx_len % CHUNK == 0 and seq % GRID_W == 0 and half % LANES == 0

    ctx_row = bsz
    cond = jnp.concatenate([c, c_ctx[None, :], jnp.zeros((8 - bsz - 1, d), F32)], axis=0)
    ada = _ada_all_layers(cond, w_ada, b_ada)
    ada3 = ada.reshape(depth * 8 * 3, 1, d)

    w_main = w_in[:, :, :n_main].astype(BF16)
    w_gate = jnp.pad(w_in[:, :, n_main:], ((0, 0), (0, 0), (0, LANES - 4 * nh))).astype(BF16)
    w_pw2_b = w_pw2.astype(BF16)
    w_out_b = w_out.astype(BF16)
    bias_g = jnp.pad(b_gate, ((0, 0), (0, LANES - 4 * nh)))

    tm_x = min(1024, seq)
    tm_c = min(1024, bsz * ctx_len)
    tn = min(1024, w_conv)
    tm_o = min(256, ctx_len)
    cw = min(256, half)

    xs = x.reshape(bsz * seq, d)
    cs = ctx.reshape(bsz * ctx_len, d)
    for l in range(depth):
        last = l == depth - 1
        row_x = lambda i, l=l, t=seq // tm_x: l * 8 + i // t
        row_c = lambda i, l=l: l * 8 + ctx_row + 0 * i
        px, gx = _in_proj(xs, ada3, row_x, g_pre[l][None], w_main[l], w_gate[l],
                          tm=tm_x, tn=tn, w_conv=w_conv, w_ml=w_ml)
        pc, gc = _in_proj(cs, ada3, row_c, g_pre[l][None], w_main[l], w_gate[l],
                          tm=tm_c, tn=tn, w_conv=w_conv, w_ml=w_ml)
        btx, rtx = _gate_prep(gx, bias_g[l][None], nh, min(1024, seq))
        btc, rtc = _gate_prep(gc, bias_g[l][None], nh, min(1024, ctx_len))
        ymx, ymc = _mlstm(px, pc, btx, rtx, btc, rtc, g_head[l][None], bsz=bsz, seq=seq,
                          ctx_len=ctx_len, nh=nh, w_conv=w_conv, w_ml=w_ml)
        ncol = half // cw
        g0 = w_conv // cw
        cv_w = _segconv(px, w_dw[l], b_dw[l][None], seg=GRID_W, rows=min(512, seq), cw=cw, ncol=ncol,
                        a_col0=0, g_col0=g0, w_col0=0)
        cv_h = _rowconv(px, w_dw[l], b_dw[l][None], seq=seq, cw=cw, ncol=ncol,
                        a_col0=ncol, g_col0=g0 + ncol, w_col0=ncol)
        row_xo = lambda i, l=l, t=seq // tm_o: l * 8 + i // t
        xs_new = _out_proj(cv_w, cv_h, 0, px, ymx, xs, ada3, row_xo, ln_g[l][None], ln_b[l][None],
                           g_post[l][None], w_pw2_b[l], w_out_b[l], tm=tm_o, w_conv=w_conv)
        if not last:
            cv_c = _segconv(pc, w_dw[l], b_dw[l][None], seg=ctx_len, rows=ctx_len, cw=cw,
                            ncol=2 * ncol, a_col0=0, g_col0=g0, w_col0=0)
            cs = _out_proj(cv_c, cv_c, 1, pc, ymc, cs, ada3, row_c, ln_g[l][None], ln_b[l][None],
                           g_post[l][None], w_pw2_b[l], w_out_b[l], tm=tm_o, w_conv=w_conv)
        xs = xs_new
    return xs.reshape(bsz, seq, d)
```

```python
import functools

import jax
import jax.numpy as jnp
from jax import lax
from jax.experimental import pallas as pl
from jax.experimental.pallas import tpu as pltpu

F32 = jnp.float32
BF16 = jnp.bfloat16

GRID_W = 64
CONV_K = 31
CONV_PAD = CONV_K // 2
EPS = 1e-6
NEG = -1e30
LANES = 128
CHUNK = 128
SEG_PAD = 16
VMEM_LIMIT = 56 * 1024 * 1024


def _cparams(sem):
    return pltpu.CompilerParams(dimension_semantics=sem, vmem_limit_bytes=VMEM_LIMIT)


def _sigmoid(x):
    return 1.0 / (1.0 + jnp.exp(-x))


def _ada_kernel(c_ref, w_ref, b_ref, o_ref):
    c = c_ref[...]
    s = (c * _sigmoid(c)).astype(BF16)
    o_ref[0] = jnp.dot(s, w_ref[0].astype(BF16), preferred_element_type=F32) + b_ref[0]


def _ada_all_layers(cond, w_ada, b_ada):
    depth, d, n = w_ada.shape
    tn = 1024 if n % 1024 == 0 else n // 3
    return pl.pallas_call(
        _ada_kernel,
        out_shape=jax.ShapeDtypeStruct((depth, 8, n), F32),
        grid=(depth, n // tn),
        in_specs=[pl.BlockSpec((8, d), lambda l, j: (0, 0)),
                  pl.BlockSpec((1, d, tn), lambda l, j: (l, 0, j)),
                  pl.BlockSpec((1, 1, tn), lambda l, j: (l, 0, j))],
        out_specs=pl.BlockSpec((1, 8, tn), lambda l, j: (l, 0, j)),
        compiler_params=_cparams(("arbitrary", "arbitrary")),
        name="ada",
    )(cond, w_ada, b_ada.reshape(depth, 1, n))


def _in_kernel(x_ref, sh_ref, sc_ref, gpre_ref, w_ref, wg_ref, p_ref, g_ref, hx_ref, *,
               tn, q_lo, q_hi, q_scale, sub):
    j = pl.program_id(1)

    @pl.when(j == 0)
    def _():
        tm = x_ref.shape[0]
        for r in range(0, tm, sub):
            x = x_ref[r:r + sub, :]
            ms = jnp.mean(x * x, axis=-1, keepdims=True)
            h = (x * lax.rsqrt(ms + EPS) * gpre_ref[...]) * (1.0 + sc_ref[0]) + sh_ref[0]
            hx_ref[r:r + sub, :] = h.astype(BF16)
        g_ref[...] = jnp.dot(hx_ref[...], wg_ref[...], preferred_element_type=F32)

    acc = jnp.dot(hx_ref[...], w_ref[...], preferred_element_type=F32)
    col = j * tn
    scale = jnp.where((col >= q_lo) & (col < q_hi), q_scale, 1.0).astype(F32)
    p_ref[...] = (acc * scale).astype(BF16)


def _in_proj(x2d, ada3, row_of_tile, g_pre, w_main, w_gate, *, tm, tn, w_conv, w_ml):
    m, d = x2d.shape
    n = w_main.shape[1]
    q_lo = 3 * w_conv
    kern = functools.partial(_in_kernel, tn=tn, q_lo=q_lo, q_hi=q_lo + w_ml,
                             q_scale=float(LANES) ** -0.5, sub=min(tm, 256))
    return pl.pallas_call(
        kern,
        out_shape=(jax.ShapeDtypeStruct((m, n), BF16), jax.ShapeDtypeStruct((m, LANES), F32)),
        grid=(m // tm, n // tn),
        in_specs=[pl.BlockSpec((tm, d), lambda i, j: (i, 0)),
                  pl.BlockSpec((1, 1, d), lambda i, j: (row_of_tile(i) * 3 + 0, 0, 0)),
                  pl.BlockSpec((1, 1, d), lambda i, j: (row_of_tile(i) * 3 + 1, 0, 0)),
                  pl.BlockSpec((1, d), lambda i, j: (0, 0)),
                  pl.BlockSpec((d, tn), lambda i, j: (0, j)),
                  pl.BlockSpec((d, LANES), lambda i, j: (0, 0))],
        out_specs=(pl.BlockSpec((tm, tn), lambda i, j: (i, j)),
                   pl.BlockSpec((tm, LANES), lambda i, j: (i, 0))),
        scratch_shapes=[pltpu.VMEM((tm, d), BF16)],
        compiler_params=_cparams(("arbitrary", "arbitrary")),
        name="in_proj",
    )(x2d, ada3, ada3, g_pre, w_main, w_gate)


def _gate_kernel(g_ref, bias_ref, bt_ref, rt_ref, ct_ref, *, nh):
    row = lax.broadcasted_iota(jnp.int32, (CHUNK, CHUNK), 0)
    col = lax.broadcasted_iota(jnp.int32, (CHUNK, CHUNK), 1)
    tril = jnp.where(row >= col, 1.0, 0.0).astype(BF16)
    triu = jnp.where(row <= col, 1.0, 0.0).astype(BF16)
    is_ff = (col >= nh) & (col < 2 * nh)
    is_fb = (col >= 3 * nh) & (col < 4 * nh)
    grow = lax.broadcasted_iota(jnp.int32, (4 * nh, CHUNK), 0)
    tok = lax.broadcasted_iota(jnp.int32, (4 * nh, CHUNK), 1)

    def csum(tri, parts):
        out = jnp.dot(tri, parts[0], preferred_element_type=F32)
        for p in parts[1:]:
            out = out + jnp.dot(tri, p, preferred_element_type=F32)
        return out

    for c in range(g_ref.shape[0] // CHUNK):
        g = g_ref[c * CHUNK:(c + 1) * CHUNK, :] + bias_ref[...]
        lf = jnp.minimum(g, 0.0) - jnp.log(1.0 + jnp.exp(-jnp.abs(g)))
        hi = lf.astype(BF16)
        r1 = lf - hi.astype(F32)
        mid = r1.astype(BF16)
        lo = (r1 - mid.astype(F32)).astype(BF16)
        parts = (hi, mid, lo)
        bmat = jnp.where(is_ff, csum(tril, parts), jnp.where(is_fb, csum(triu, parts), 0.0))
        li = pltpu.roll(g, nh, axis=1)
        rmat = li - bmat
        bt_ref[c] = bmat.T[0:4 * nh, :]
        rt = rmat.T[0:4 * nh, :]
        rt_ref[c] = rt
        cf = rt
        cb = rt
        sh = 1
        while sh < CHUNK:
            cf = jnp.maximum(cf, jnp.where(tok >= sh, pltpu.roll(cf, sh, axis=1), NEG))
            cb = jnp.maximum(cb, jnp.where(tok < CHUNK - sh, pltpu.roll(cb, CHUNK - sh, axis=1), NEG))
            sh *= 2
        ct_ref[c] = jnp.where(grow < 2 * nh, cf, cb)


def _gate_prep(g2d, bias, nh, rows):
    m = g2d.shape[0]
    nc = rows // CHUNK
    out = jax.ShapeDtypeStruct((m // CHUNK, 4 * nh, CHUNK), F32)
    ospec = pl.BlockSpec((nc, 4 * nh, CHUNK), lambda i: (i, 0, 0))
    return pl.pallas_call(
        functools.partial(_gate_kernel, nh=nh),
        out_shape=(out, out, out),
        grid=(m // rows,),
        in_specs=[pl.BlockSpec((rows, LANES), lambda i: (i, 0)),
                  pl.BlockSpec((1, LANES), lambda i: (0, 0))],
        out_specs=(ospec, ospec, ospec),
        compiler_params=_cparams(("arbitrary",)),
        name="gate_prep",
    )(g2d, bias)


def _mlstm_kernel(qx_ref, kx_ref, vx_ref, ox_ref, zx_ref, qc_ref, kc_ref, vc_ref, oc_ref, zc_ref,
                  btx_ref, rtx_ref, ctx_ref, btc_ref, rtc_ref, ctc_ref, gh_ref, yx_ref, yc_ref,
                  kt_s, s_s, a_s, kv_s, bb_s, cm_s, sc_s, cp_s, mp_s, st_s, *, nh, unroll):
    h_idx = pl.program_id(1)
    row = lax.broadcasted_iota(jnp.int32, (CHUNK, CHUNK), 0)
    col = lax.broadcasted_iota(jnp.int32, (CHUNK, CHUNK), 1)
    masks = (row >= col, row <= col)
    ones_blk = jnp.ones((CHUNK, LANES), BF16)
    ncx = qx_ref.shape[0] // CHUNK
    ncc = qc_ref.shape[0] // CHUNK
    nct = ncc + ncx
    g_rows = (nh + h_idx, 3 * nh + h_idx)
    last = (CHUNK - 1, 0)

    def stage1(nc, c0, k_ref, bt_ref, ct_ref):
        def body(c, _):
            r0 = pl.multiple_of(c * CHUNK, CHUNK)
            rs = pl.multiple_of((c0 + c) * CHUNK, CHUNK)
            kt_s[c0 + c] = k_ref[pl.ds(r0, CHUNK), :].astype(F32).T.astype(BF16)
            for d in range(2):
                b_row = bt_ref[c, pl.ds(g_rows[d], 1), :]
                c_row = ct_ref[c, pl.ds(g_rows[d], 1), :]
                bb_s[d, pl.ds(rs, CHUNK), :] = jnp.broadcast_to(b_row, (CHUNK, CHUNK)).T
                cm_s[d, pl.ds(rs, CHUNK), :] = jnp.broadcast_to(c_row, (CHUNK, CHUNK)).T
                b_last = b_row[:, last[d]:last[d] + 1]
                sc_s[d, c0 + c, 0:1, :] = jnp.broadcast_to(b_last, (1, LANES))
                sc_s[d, c0 + c, 1:2, :] = jnp.broadcast_to(
                    b_last + c_row[:, last[d]:last[d] + 1], (1, LANES))
            return 0
        lax.fori_loop(0, nc, body, 0, unroll=unroll)

    def stage2(nc, c0, q_ref, v_ref, rt_ref):
        def body(c, _):
            r0 = pl.multiple_of(c * CHUNK, CHUNK)
            rs = pl.multiple_of((c0 + c) * CHUNK, CHUNK)
            ktb = kt_s[c0 + c]
            s_s[pl.ds(rs, CHUNK), :] = jnp.dot(q_ref[pl.ds(r0, CHUNK), :], ktb,
                                               preferred_element_type=F32)
            vaug = jnp.concatenate([v_ref[pl.ds(r0, CHUNK), :], ones_blk], axis=1)
            kt = ktb.astype(F32)
            for d in range(2):
                gl = sc_s[d, c0 + c, 0:1, :] + rt_ref[c, pl.ds(g_rows[d], 1), :]
                kws = (kt * jnp.exp(gl - sc_s[d, c0 + c, 1:2, :])).astype(BF16)
                kv_s[d, c0 + c] = jnp.dot(kws, vaug, preferred_element_type=F32)
            return 0
        lax.fori_loop(0, nc, body, 0, unroll=unroll)

    def stage3(nc, c0, v_ref, rt_ref):
        def body(c, _):
            r0 = pl.multiple_of(c * CHUNK, CHUNK)
            rs = pl.multiple_of((c0 + c) * CHUNK, CHUNK)
            s = s_s[pl.ds(rs, CHUNK), :]
            vaug = jnp.concatenate([v_ref[pl.ds(r0, CHUNK), :], ones_blk], axis=1)
            for d in range(2):
                r_row = rt_ref[c, pl.ds(g_rows[d], 1), :]
                arg = jnp.where(masks[d], r_row - cm_s[d, pl.ds(rs, CHUNK), :], NEG)
                pm = (jnp.exp(arg) * s).astype(BF16)
                a_s[d, pl.ds(rs, CHUNK), :] = jnp.dot(pm, vaug, preferred_element_type=F32)
            return 0
        lax.fori_loop(0, nc, body, 0, unroll=unroll)

    stage1(ncc, 0, kc_ref, btc_ref, ctc_ref)
    stage1(ncx, ncc, kx_ref, btx_ref, ctx_ref)
    stage2(ncc, 0, qc_ref, vc_ref, rtc_ref)
    stage2(ncx, ncc, qx_ref, vx_ref, rtx_ref)
    stage3(ncc, 0, vc_ref, rtc_ref)
    stage3(ncx, ncc, vx_ref, rtx_ref)

    st_s[...] = jnp.zeros_like(st_s)

    def scan(j, ms):
        new = []
        for d in range(2):
            g = j if d == 0 else jnp.where(j < ncc, ncc - 1 - j, nct - 1 - j + ncc)
            m_prev = ms[d]
            c_prev = st_s[d]
            cp_s[d, g] = c_prev.astype(BF16)
            mp_s[d, g] = jnp.broadcast_to(m_prev, (8, LANES))
            b_last = sc_s[d, g, 0:1, :]
            mg = sc_s[d, g, 1:2, :]
            m_new = jnp.maximum(b_last + m_prev, mg)
            a = jnp.exp(b_last + m_prev - m_new)
            w = jnp.exp(mg - m_new)
            st_s[d] = (jnp.concatenate([a, a], axis=1) * c_prev
                       + jnp.concatenate([w, w], axis=1) * kv_s[d, g])
            new.append(m_new)
        return tuple(new)

    m0 = jnp.full((1, LANES), NEG, F32)
    lax.fori_loop(0, nct, scan, (m0, m0))

    def phase_c(nc, c0, q_ref, o_ref, z_ref, y_ref):
        def body(c, _):
            r0 = pl.multiple_of(c * CHUNK, CHUNK)
            rs = pl.multiple_of((c0 + c) * CHUNK, CHUNK)
            q = q_ref[pl.ds(r0, CHUNK), :]
            h = None
            for d in range(2):
                qc = jnp.dot(q, cp_s[d, c0 + c], preferred_element_type=F32)
                a_in = a_s[d, pl.ds(rs, CHUNK), :]
                cm = cm_s[d, pl.ds(rs, CHUNK), :]
                m_prev = mp_s[d, c0 + c, 0:1, :]
                t = m_prev - cm
                w1 = jnp.exp(jnp.minimum(t, 0.0))
                w2 = jnp.exp(jnp.minimum(-t, 0.0))
                m_t = bb_s[d, pl.ds(rs, CHUNK), :] + jnp.maximum(m_prev, cm)
                num = w1 * qc[:, 0:LANES] + w2 * a_in[:, 0:LANES]
                den = w1 * qc[:, LANES:] + w2 * a_in[:, LANES:]
                hd = num / jnp.maximum(jnp.abs(den), jnp.exp(-m_t))
                h = hd if h is None else h + hd
            mu = jnp.mean(h, axis=1, keepdims=True)
            hc = h - mu
            hn = hc * lax.rsqrt(jnp.mean(hc * hc, axis=1, keepdims=True) + EPS) * gh_ref[...]
            z = z_ref[pl.ds(r0, CHUNK), :].astype(F32)
            y = hn * _sigmoid(o_ref[pl.ds(r0, CHUNK), :].astype(F32)) * (z * _sigmoid(z))
            y_ref[pl.ds(r0, CHUNK), :] = y.astype(BF16)
            return 0
        lax.fori_loop(0, nc, body, 0, unroll=unroll)

    phase_c(ncc, 0, qc_ref, oc_ref, zc_ref, yc_ref)
    phase_c(ncx, ncc, qx_ref, ox_ref, zx_ref, yx_ref)


def _mlstm(px, pc, gates_x, gates_c, g_head, *, bsz, seq, ctx_len, nh, w_conv, w_ml):
    dh = LANES
    cb = 3 * w_conv // dh
    hb = w_ml // dh

    def colspec(rows, k):
        return pl.BlockSpec((rows, dh), lambda b, h: (b, cb + k * hb + h))

    def gspec(rows):
        return pl.BlockSpec((rows // CHUNK, 4 * nh, CHUNK), lambda b, h: (b, 0, 0))

    tot = seq + ctx_len
    nct = tot // CHUNK
    return pl.pallas_call(
        functools.partial(_mlstm_kernel, nh=nh, unroll=4),
        out_shape=(jax.ShapeDtypeStruct((bsz * seq, w_ml), BF16),
                   jax.ShapeDtypeStruct((bsz * ctx_len, w_ml), BF16)),
        grid=(bsz, nh),
        in_specs=[colspec(seq, 0), colspec(seq, 1), colspec(seq, 2), colspec(seq, 3), colspec(seq, 4),
                  colspec(ctx_len, 0), colspec(ctx_len, 1), colspec(ctx_len, 2), colspec(ctx_len, 3),
                  colspec(ctx_len, 4),
                  gspec(seq), gspec(seq), gspec(seq), gspec(ctx_len), gspec(ctx_len), gspec(ctx_len),
                  pl.BlockSpec((1, dh), lambda b, h: (0, h))],
        out_specs=(pl.BlockSpec((seq, dh), lambda b, h: (b, h)),
                   pl.BlockSpec((ctx_len, dh), lambda b, h: (b, h))),
        scratch_shapes=[pltpu.VMEM((nct, dh, CHUNK), BF16),
                        pltpu.VMEM((tot, CHUNK), F32),
                        pltpu.VMEM((2, tot, 2 * dh), F32),
                        pltpu.VMEM((2, nct, dh, 2 * dh), F32),
                        pltpu.VMEM((2, tot, dh), F32),
                        pltpu.VMEM((2, tot, dh), F32),
                        pltpu.VMEM((2, nct, 8, LANES), F32),
                        pltpu.VMEM((2, nct, dh, 2 * dh), BF16),
                        pltpu.VMEM((2, nct, 8, LANES), F32),
                        pltpu.VMEM((2, dh, 2 * dh), F32)],
        compiler_params=_cparams(("arbitrary", "arbitrary")),
        name="mlstm",
    )(px, px, px, px, px, pc, pc, pc, pc, pc, *gates_x, *gates_c, g_head)


def _glu(a_ref, g_ref, r0, rows):
    a = a_ref[r0:r0 + rows, :].astype(F32)
    return a * _sigmoid(g_ref[r0:r0 + rows, :].astype(F32))


def _segconv_kernel(a_ref, g_ref, w_ref, b_ref, o_ref, pad_ref, *, seg, sub):
    rows, cw = a_ref.shape
    stride = seg + 2 * SEG_PAD
    zeros = jnp.zeros((SEG_PAD, cw), F32)
    for s in range(rows // seg):
        pad_ref[s * stride:s * stride + SEG_PAD, :] = zeros
        pad_ref[s * stride + SEG_PAD:s * stride + SEG_PAD + seg, :] = _glu(a_ref, g_ref, s * seg, seg)
        pad_ref[s * stride + SEG_PAD + seg:(s + 1) * stride, :] = zeros
    for s in range(rows // seg):
        for t0 in range(0, seg, sub):
            start = s * stride + SEG_PAD + t0 - CONV_PAD
            acc = jnp.zeros((sub, cw), F32) + b_ref[...]
            for k in range(CONV_K):
                acc = acc + pad_ref[start + k:start + k + sub, :] * w_ref[k:k + 1, :]
            o_ref[s * seg + t0:s * seg + t0 + sub, :] = acc


def _segconv(p, w_dw, b_dw, *, seg, rows, cw, ncol, a_col0, g_col0, w_col0):
    m = p.shape[0]
    stride = seg + 2 * SEG_PAD
    return pl.pallas_call(
        functools.partial(_segconv_kernel, seg=seg, sub=min(seg, 64)),
        out_shape=jax.ShapeDtypeStruct((m, ncol * cw), F32),
        grid=(m // rows, ncol),
        in_specs=[pl.BlockSpec((rows, cw), lambda i, j: (i, a_col0 + j)),
                  pl.BlockSpec((rows, cw), lambda i, j: (i, g_col0 + j)),
                  pl.BlockSpec((CONV_K, cw), lambda i, j: (0, w_col0 + j)),
                  pl.BlockSpec((1, cw), lambda i, j: (0, w_col0 + j))],
        out_specs=pl.BlockSpec((rows, cw), lambda i, j: (i, j)),
        scratch_shapes=[pltpu.VMEM((rows // seg * stride, cw), F32)],
        compiler_params=_cparams(("arbitrary", "arbitrary")),
        name="segconv",
    )(p, p, w_dw, b_dw)


def _rowconv_kernel(a_ref, g_ref, w_ref, b_ref, o_ref, pad_ref, *, nrows):
    cw = a_ref.shape[1]
    edge = CONV_PAD * GRID_W
    pad_ref[0:edge, :] = jnp.zeros((edge, cw), F32)
    pad_ref[edge + nrows * GRID_W:2 * edge + nrows * GRID_W, :] = jnp.zeros((edge, cw), F32)
    for r in range(nrows):
        pad_ref[edge + r * GRID_W:edge + (r + 1) * GRID_W, :] = _glu(a_ref, g_ref, r * GRID_W, GRID_W)

    def body(r, _):
        acc = jnp.zeros((GRID_W, cw), F32) + b_ref[...]
        for k in range(CONV_K):
            src = pl.multiple_of((r + k) * GRID_W, GRID_W)
            acc = acc + pad_ref[pl.ds(src, GRID_W), :] * w_ref[k:k + 1, :]
        o_ref[pl.ds(pl.multiple_of(r * GRID_W, GRID_W), GRID_W), :] = acc
        return 0
    lax.fori_loop(0, nrows, body, 0)


def _rowconv(p, w_dw, b_dw, *, seq, cw, ncol, a_col0, g_col0, w_col0):
    m = p.shape[0]
    nrows = seq // GRID_W
    return pl.pallas_call(
        functools.partial(_rowconv_kernel, nrows=nrows),
        out_shape=jax.ShapeDtypeStruct((m, ncol * cw), F32),
        grid=(m // seq, ncol),
        in_specs=[pl.BlockSpec((seq, cw), lambda i, j: (i, a_col0 + j)),
                  pl.BlockSpec((seq, cw), lambda i, j: (i, g_col0 + j)),
                  pl.BlockSpec((CONV_K, cw), lambda i, j: (0, w_col0 + j)),
                  pl.BlockSpec((1, cw), lambda i, j: (0, w_col0 + j))],
        out_specs=pl.BlockSpec((seq, cw), lambda i, j: (i, j)),
        scratch_shapes=[pltpu.VMEM((seq + 2 * CONV_PAD * GRID_W, cw), F32)],
        compiler_params=_cparams(("arbitrary", "arbitrary")),
        name="rowconv",
    )(p, p, w_dw, b_dw)


def _out_kernel(cva_ref, cvb_ref, z_ref, ym_ref, x_ref, gt_ref, lng_ref, lnb_ref, gpost_ref,
                wpw_ref, wout_ref, o_ref):
    w_conv = wpw_ref.shape[0]
    u = jnp.concatenate([cva_ref[...], cvb_ref[...]], axis=1)
    mu = jnp.mean(u, axis=-1, keepdims=True)
    uc = u - mu
    r = uc * lax.rsqrt(jnp.mean(uc * uc, axis=-1, keepdims=True) + EPS) * lng_ref[...] + lnb_ref[...]
    t = jnp.dot((r * _sigmoid(r)).astype(BF16), wpw_ref[...], preferred_element_type=F32)
    z = z_ref[...].astype(F32)
    yc = (t * (z * _sigmoid(z))).astype(BF16)
    out = (jnp.dot(yc, wout_ref[0:w_conv, :], preferred_element_type=F32)
           + jnp.dot(ym_ref[...], wout_ref[w_conv:, :], preferred_element_type=F32))
    ms = jnp.mean(out * out, axis=-1, keepdims=True)
    o_ref[...] = x_ref[...] + gt_ref[0] * (out * lax.rsqrt(ms + EPS) * gpost_ref[...])


def _out_proj(cva, cvb, cvb_col, p, ym, x2d, ada3, row_of_tile, ln_g, ln_b, g_post, w_pw2, w_out,
              *, tm, w_conv):
    m, d = x2d.shape
    half = w_conv // 2
    w_mix = w_out.shape[0]
    return pl.pallas_call(
        _out_kernel,
        out_shape=jax.ShapeDtypeStruct((m, d), F32),
        grid=(m // tm,),
        in_specs=[pl.BlockSpec((tm, half), lambda i: (i, 0)),
                  pl.BlockSpec((tm, half), lambda i: (i, cvb_col)),
                  pl.BlockSpec((tm, w_conv), lambda i: (i, 2)),
                  pl.BlockSpec((tm, w_mix - w_conv), lambda i: (i, 0)),
                  pl.BlockSpec((tm, d), lambda i: (i, 0)),
                  pl.BlockSpec((1, 1, d), lambda i: (row_of_tile(i) * 3 + 2, 0, 0)),
                  pl.BlockSpec((1, w_conv), lambda i: (0, 0)),
                  pl.BlockSpec((1, w_conv), lambda i: (0, 0)),
                  pl.BlockSpec((1, d), lambda i: (0, 0)),
                  pl.BlockSpec((w_conv, w_conv), lambda i: (0, 0)),
                  pl.BlockSpec((w_mix, d), lambda i: (0, 0))],
        out_specs=pl.BlockSpec((tm, d), lambda i: (i, 0)),
        compiler_params=_cparams(("arbitrary",)),
        name="out_proj",
    )(cva, cvb, p, ym, x2d, ada3, ln_g, ln_b, g_post, w_pw2, w_out)


def kernel(x, c, ctx, c_ctx, w_ada, b_ada, g_pre, g_post, w_in, b_gate, w_dw, b_dw, ln_g, ln_b, w_pw2,
           g_head, w_out):
    bsz, seq, d = x.shape
    ctx_len = ctx.shape[1]
    depth = w_ada.shape[0]
    w_conv = w_dw.shape[-1]
    w_ml = g_head.shape[-1]
    nh = b_gate.shape[-1] // 4
    n_main = 3 * w_conv + 5 * w_ml
    half = w_conv // 2
    assert w_ml == nh * LANES and 4 * nh <= LANES and w_conv == w_ml
    assert seq % CHUNK == 0 and ctx_len % CHUNK == 0 and seq % GRID_W == 0 and half % LANES == 0

    ctx_row = bsz
    cond = jnp.concatenate([c, c_ctx[None, :], jnp.zeros((8 - bsz - 1, d), F32)], axis=0)
    ada = _ada_all_layers(cond, w_ada, b_ada)
    ada3 = ada.reshape(depth * 8 * 3, 1, d)

    w_main = w_in[:, :, :n_main].astype(BF16)
    w_gate = jnp.pad(w_in[:, :, n_main:], ((0, 0), (0, 0), (0, LANES - 4 * nh))).astype(BF16)
    w_pw2_b = w_pw2.astype(BF16)
    w_out_b = w_out.astype(BF16)
    bias_g = jnp.pad(b_gate, ((0, 0), (0, LANES - 4 * nh)))

    tm_x = min(1024, seq)
    tm_c = min(1024, bsz * ctx_len)
    tn = min(1024, w_conv)
    tm_o = min(256, ctx_len)
    cw = min(256, half)

    xs = x.reshape(bsz * seq, d)
    cs = ctx.reshape(bsz * ctx_len, d)
    for l in range(depth):
        last = l == depth - 1
        row_x = lambda i, l=l, t=seq // tm_x: l * 8 + i // t
        row_c = lambda i, l=l: l * 8 + ctx_row + 0 * i
        px, gx = _in_proj(xs, ada3, row_x, g_pre[l][None], w_main[l], w_gate[l],
                          tm=tm_x, tn=tn, w_conv=w_conv, w_ml=w_ml)
        pc, gc = _in_proj(cs, ada3, row_c, g_pre[l][None], w_main[l], w_gate[l],
                          tm=tm_c, tn=tn, w_conv=w_conv, w_ml=w_ml)
        gates_x = _gate_prep(gx, bias_g[l][None], nh, min(1024, seq))
        gates_c = _gate_prep(gc, bias_g[l][None], nh, min(1024, ctx_len))
        ymx, ymc = _mlstm(px, pc, gates_x, gates_c, g_head[l][None], bsz=bsz, seq=seq,
                          ctx_len=ctx_len, nh=nh, w_conv=w_conv, w_ml=w_ml)
        ncol = half // cw
        g0 = w_conv // cw
        cv_w = _segconv(px, w_dw[l], b_dw[l][None], seg=GRID_W, rows=min(512, seq), cw=cw, ncol=ncol,
                        a_col0=0, g_col0=g0, w_col0=0)
        cv_h = _rowconv(px, w_dw[l], b_dw[l][None], seq=seq, cw=cw, ncol=ncol,
                        a_col0=ncol, g_col0=g0 + ncol, w_col0=ncol)
        row_xo = lambda i, l=l, t=seq // tm_o: l * 8 + i // t
        xs_new = _out_proj(cv_w, cv_h, 0, px, ymx, xs, ada3, row_xo, ln_g[l][None], ln_b[l][None],
                           g_post[l][None], w_pw2_b[l], w_out_b[l], tm=tm_o, w_conv=w_conv)
        if not last:
            cv_c = _segconv(pc, w_dw[l], b_dw[l][None], seg=ctx_len, rows=ctx_len, cw=cw,
                            ncol=2 * ncol, a_col0=0, g_col0=g0, w_col0=0)
            cs = _out_proj(cv_c, cv_c, 1, pc, ymc, cs, ada3, row_c, ln_g[l][None], ln_b[l][None],
                           g_post[l][None], w_pw2_b[l], w_out_b[l], tm=tm_o, w_conv=w_conv)
        xs = xs_new
    return xs.reshape(bsz, seq, d)
```

```python
import functools

import jax
import jax.numpy as jnp
from jax import lax
from jax.experimental import pallas as pl
from jax.experimental.pallas import tpu as pltpu

F32 = jnp.float32
BF16 = jnp.bfloat16

GRID_W = 64
CONV_K = 31
CONV_PAD = CONV_K // 2
EPS = 1e-6
NEG = -1e30
LANES = 128
CHUNK = 128
SEG_PAD = 16
VMEM_LIMIT = 56 * 1024 * 1024


def _cparams(sem):
    return pltpu.CompilerParams(dimension_semantics=sem, vmem_limit_bytes=VMEM_LIMIT)


def _sigmoid(x):
    return 1.0 / (1.0 + jnp.exp(-x))


def _ada_kernel(c_ref, w_ref, b_ref, o_ref):
    c = c_ref[...]
    s = (c * _sigmoid(c)).astype(BF16)
    o_ref[0] = jnp.dot(s, w_ref[0].astype(BF16), preferred_element_type=F32) + b_ref[0]


def _ada_all_layers(cond, w_ada, b_ada):
    depth, d, n = w_ada.shape
    tn = 1024 if n % 1024 == 0 else n // 3
    return pl.pallas_call(
        _ada_kernel,
        out_shape=jax.ShapeDtypeStruct((depth, 8, n), F32),
        grid=(depth, n // tn),
        in_specs=[pl.BlockSpec((8, d), lambda l, j: (0, 0)),
                  pl.BlockSpec((1, d, tn), lambda l, j: (l, 0, j)),
                  pl.BlockSpec((1, 1, tn), lambda l, j: (l, 0, j))],
        out_specs=pl.BlockSpec((1, 8, tn), lambda l, j: (l, 0, j)),
        compiler_params=_cparams(("arbitrary", "arbitrary")),
        name="ada",
    )(cond, w_ada, b_ada.reshape(depth, 1, n))


def _in_kernel(x_ref, sh_ref, sc_ref, gpre_ref, w_ref, wg_ref, p_ref, g_ref, *rest,
               tn, q_lo, q_hi, q_scale, sub):
    hx_ref = rest[-1]
    j = pl.program_id(1)

    @pl.when(j == 0)
    def _():
        tm = x_ref.shape[0]
        for r in range(0, tm, sub):
            x = x_ref[r:r + sub, :]
            ms = jnp.mean(x * x, axis=-1, keepdims=True)
            h = (x * lax.rsqrt(ms + EPS) * gpre_ref[...]) * (1.0 + sc_ref[0]) + sh_ref[0]
            hx_ref[r:r + sub, :] = h.astype(BF16)
        g_ref[...] = jnp.dot(hx_ref[...], wg_ref[...], preferred_element_type=F32)

    w = w_ref[...]
    if len(rest) == 2:
        w = w.astype(BF16)
        rest[0][...] = w
    acc = jnp.dot(hx_ref[...], w, preferred_element_type=F32)
    col = j * tn
    scale = jnp.where((col >= q_lo) & (col < q_hi), q_scale, 1.0).astype(F32)
    p_ref[...] = (acc * scale).astype(BF16)


def _in_proj(x2d, ada3, row_of_tile, g_pre, w, w_gate, *, tm, tn, w_conv, w_ml, layer=None):
    m, d = x2d.shape
    n = 3 * w_conv + 5 * w_ml
    q_lo = 3 * w_conv
    kern = functools.partial(_in_kernel, tn=tn, q_lo=q_lo, q_hi=q_lo + w_ml,
                             q_scale=float(LANES) ** -0.5, sub=min(tm, 256))
    out_shape = [jax.ShapeDtypeStruct((m, n), BF16), jax.ShapeDtypeStruct((m, LANES), F32)]
    out_specs = [pl.BlockSpec((tm, tn), lambda i, j: (i, j)),
                 pl.BlockSpec((tm, LANES), lambda i, j: (i, 0))]
    if layer is None:
        w_spec = pl.BlockSpec((d, tn), lambda i, j: (0, j))
    else:
        assert m == tm
        w_spec = pl.BlockSpec((None, d, tn), lambda i, j: (layer, 0, j))
        out_shape.append(jax.ShapeDtypeStruct((d, n), BF16))
        out_specs.append(pl.BlockSpec((d, tn), lambda i, j: (0, j)))
    return pl.pallas_call(
        kern,
        out_shape=tuple(out_shape),
        grid=(m // tm, n // tn),
        in_specs=[pl.BlockSpec((tm, d), lambda i, j: (i, 0)),
                  pl.BlockSpec((1, 1, d), lambda i, j: (row_of_tile(i) * 3 + 0, 0, 0)),
                  pl.BlockSpec((1, 1, d), lambda i, j: (row_of_tile(i) * 3 + 1, 0, 0)),
                  pl.BlockSpec((1, d), lambda i, j: (0, 0)),
                  w_spec,
                  pl.BlockSpec((d, LANES), lambda i, j: (0, 0))],
        out_specs=tuple(out_specs),
        scratch_shapes=[pltpu.VMEM((tm, d), BF16)],
        compiler_params=_cparams(("arbitrary", "arbitrary")),
        name="in_proj",
    )(x2d, ada3, ada3, g_pre, w, w_gate)


def _gate_kernel(g_ref, bias_ref, bt_ref, rt_ref, ct_ref, *, nh):
    row = lax.broadcasted_iota(jnp.int32, (CHUNK, CHUNK), 0)
    col = lax.broadcasted_iota(jnp.int32, (CHUNK, CHUNK), 1)
    tril = jnp.where(row >= col, 1.0, 0.0).astype(BF16)
    triu = jnp.where(row <= col, 1.0, 0.0).astype(BF16)
    is_ff = (col >= nh) & (col < 2 * nh)
    is_fb = (col >= 3 * nh) & (col < 4 * nh)
    grow = lax.broadcasted_iota(jnp.int32, (4 * nh, CHUNK), 0)
    tok = lax.broadcasted_iota(jnp.int32, (4 * nh, CHUNK), 1)

    def csum(tri, parts):
        out = jnp.dot(tri, parts[0], preferred_element_type=F32)
        for p in parts[1:]:
            out = out + jnp.dot(tri, p, preferred_element_type=F32)
        return out

    for c in range(g_ref.shape[0] // CHUNK):
        g = g_ref[c * CHUNK:(c + 1) * CHUNK, :] + bias_ref[...]
        lf = jnp.minimum(g, 0.0) - jnp.log(1.0 + jnp.exp(-jnp.abs(g)))
        hi = lf.astype(BF16)
        r1 = lf - hi.astype(F32)
        mid = r1.astype(BF16)
        lo = (r1 - mid.astype(F32)).astype(BF16)
        parts = (hi, mid, lo)
        bmat = jnp.where(is_ff, csum(tril, parts), jnp.where(is_fb, csum(triu, parts), 0.0))
        li = pltpu.roll(g, nh, axis=1)
        rmat = li - bmat
        bt_ref[c] = bmat.T[0:4 * nh, :]
        rt = rmat.T[0:4 * nh, :]
        rt_ref[c] = rt
        cf = rt
        cb = rt
        sh = 1
        while sh < CHUNK:
            cf = jnp.maximum(cf, jnp.where(tok >= sh, pltpu.roll(cf, sh, axis=1), NEG))
            cb = jnp.maximum(cb, jnp.where(tok < CHUNK - sh, pltpu.roll(cb, CHUNK - sh, axis=1), NEG))
            sh *= 2
        ct_ref[c] = jnp.where(grow < 2 * nh, cf, cb)


def _gate_prep(g2d, bias, nh, rows):
    m = g2d.shape[0]
    nc = rows // CHUNK
    out = jax.ShapeDtypeStruct((m // CHUNK, 4 * nh, CHUNK), F32)
    ospec = pl.BlockSpec((nc, 4 * nh, CHUNK), lambda i: (i, 0, 0))
    return pl.pallas_call(
        functools.partial(_gate_kernel, nh=nh),
        out_shape=(out, out, out),
        grid=(m // rows,),
        in_specs=[pl.BlockSpec((rows, LANES), lambda i: (i, 0)),
                  pl.BlockSpec((1, LANES), lambda i: (0, 0))],
        out_specs=(ospec, ospec, ospec),
        compiler_params=_cparams(("arbitrary",)),
        name="gate_prep",
    )(g2d, bias)


def _mlstm_kernel(qx_ref, kx_ref, vx_ref, ox_ref, zx_ref, qc_ref, kc_ref, vc_ref, oc_ref, zc_ref,
                  btx_ref, rtx_ref, ctx_ref, btc_ref, rtc_ref, ctc_ref, gh_ref, yx_ref, yc_ref,
                  s_s, kv_s, sc_s, cp_s, mp_s, st_s, *, nh, unroll):
    h_idx = pl.program_id(1)
    row = lax.broadcasted_iota(jnp.int32, (CHUNK, CHUNK), 0)
    col = lax.broadcasted_iota(jnp.int32, (CHUNK, CHUNK), 1)
    masks = (row >= col, row <= col)
    ones_blk = jnp.ones((CHUNK, LANES), BF16)
    ncx = qx_ref.shape[0] // CHUNK
    ncc = qc_ref.shape[0] // CHUNK
    nct = ncc + ncx
    g_rows = (nh + h_idx, 3 * nh + h_idx)
    last = (CHUNK - 1, 0)

    def loop_a(nc, c0, q_ref, k_ref, v_ref, bt_ref, rt_ref, ct_ref):
        def body(c, _):
            r0 = pl.multiple_of(c * CHUNK, CHUNK)
            rs = pl.multiple_of((c0 + c) * CHUNK, CHUNK)
            kt = k_ref[pl.ds(r0, CHUNK), :].astype(F32).T
            s_s[pl.ds(rs, CHUNK), :] = jnp.dot(q_ref[pl.ds(r0, CHUNK), :], kt.astype(BF16),
                                               preferred_element_type=F32)
            vaug = jnp.concatenate([v_ref[pl.ds(r0, CHUNK), :], ones_blk], axis=1)
            for d in range(2):
                b_last = bt_ref[c, pl.ds(g_rows[d], 1), last[d]:last[d] + 1]
                mg = b_last + ct_ref[c, pl.ds(g_rows[d], 1), last[d]:last[d] + 1]
                gl = b_last + rt_ref[c, pl.ds(g_rows[d], 1), :]
                kws = (kt * jnp.exp(gl - mg)).astype(BF16)
                kv_s[d, c0 + c] = jnp.dot(kws, vaug, preferred_element_type=F32)
                sc_s[d, c0 + c, 0:1, :] = jnp.broadcast_to(b_last, (1, LANES))
                sc_s[d, c0 + c, 1:2, :] = jnp.broadcast_to(mg, (1, LANES))
            return 0
        lax.fori_loop(0, nc, body, 0, unroll=unroll)

    loop_a(ncc, 0, qc_ref, kc_ref, vc_ref, btc_ref, rtc_ref, ctc_ref)
    loop_a(ncx, ncc, qx_ref, kx_ref, vx_ref, btx_ref, rtx_ref, ctx_ref)

    st_s[...] = jnp.zeros_like(st_s)

    def scan(j, ms):
        new = []
        for d in range(2):
            g = j if d == 0 else jnp.where(j < ncc, ncc - 1 - j, nct - 1 - j + ncc)
            m_prev = ms[d]
            c_prev = st_s[d]
            cp_s[d, g] = c_prev.astype(BF16)
            mp_s[d, g] = jnp.broadcast_to(m_prev, (8, LANES))
            b_last = sc_s[d, g, 0:1, :]
            mg = sc_s[d, g, 1:2, :]
            m_new = jnp.maximum(b_last + m_prev, mg)
            a = jnp.exp(b_last + m_prev - m_new)
            w = jnp.exp(mg - m_new)
            st_s[d] = (jnp.concatenate([a, a], axis=1) * c_prev
                       + jnp.concatenate([w, w], axis=1) * kv_s[d, g])
            new.append(m_new)
        return tuple(new)

    m0 = jnp.full((1, LANES), NEG, F32)
    lax.fori_loop(0, nct, scan, (m0, m0))

    def loop_b(nc, c0, q_ref, v_ref, o_ref, z_ref, bt_ref, rt_ref, ct_ref, y_ref):
        def body(c, _):
            r0 = pl.multiple_of(c * CHUNK, CHUNK)
            rs = pl.multiple_of((c0 + c) * CHUNK, CHUNK)
            q = q_ref[pl.ds(r0, CHUNK), :].astype(F32)
            s = s_s[pl.ds(rs, CHUNK), :]
            vaug = jnp.concatenate([v_ref[pl.ds(r0, CHUNK), :], ones_blk], axis=1)
            h = None
            for d in range(2):
                c_row = ct_ref[c, pl.ds(g_rows[d], 1), :]
                r_row = rt_ref[c, pl.ds(g_rows[d], 1), :]
                m_prev = mp_s[d, c0 + c, 0:1, :]
                floor_bc = jnp.broadcast_to(
                    jnp.exp(-(bt_ref[c, pl.ds(g_rows[d], 1), :] + jnp.maximum(c_row, m_prev))),
                    (CHUNK, CHUNK)).T
                cm_bc = jnp.broadcast_to(c_row, (CHUNK, CHUNK)).T
                mm = jnp.maximum(cm_bc, m_prev)
                qw = (q * jnp.exp(m_prev - mm)).astype(BF16)
                pm = (jnp.exp(jnp.where(masks[d], r_row - mm, NEG)) * s).astype(BF16)
                nd = jnp.dot(jnp.concatenate([qw, pm], axis=1),
                             jnp.concatenate([cp_s[d, c0 + c], vaug], axis=0),
                             preferred_element_type=F32)
                hd = nd[:, 0:LANES] / jnp.maximum(jnp.abs(nd[:, LANES:]), floor_bc)
                h = hd if h is None else h + hd
            mu = jnp.mean(h, axis=1, keepdims=True)
            hc = h - mu
            hn = hc * lax.rsqrt(jnp.mean(hc * hc, axis=1, keepdims=True) + EPS) * gh_ref[...]
            z = z_ref[pl.ds(r0, CHUNK), :].astype(F32)
            o = o_ref[pl.ds(r0, CHUNK), :].astype(F32)
            y = hn * (z / ((1.0 + jnp.exp(-o)) * (1.0 + jnp.exp(-z))))
            y_ref[pl.ds(r0, CHUNK), :] = y.astype(BF16)
            return 0
        lax.fori_loop(0, nc, body, 0, unroll=unroll)

    loop_b(ncc, 0, qc_ref, vc_ref, oc_ref, zc_ref, btc_ref, rtc_ref, ctc_ref, yc_ref)
    loop_b(ncx, ncc, qx_ref, vx_ref, ox_ref, zx_ref, btx_ref, rtx_ref, ctx_ref, yx_ref)


def _mlstm(px, pc, gates_x, gates_c, g_head, *, bsz, seq, ctx_len, nh, w_conv, w_ml):
    dh = LANES
    cb = 3 * w_conv // dh
    hb = w_ml // dh

    def colspec(rows, k):
        return pl.BlockSpec((rows, dh), lambda b, h: (b, cb + k * hb + h))

    def gspec(rows):
        return pl.BlockSpec((rows // CHUNK, 4 * nh, CHUNK), lambda b, h: (b, 0, 0))

    tot = seq + ctx_len
    nct = tot // CHUNK
    return pl.pallas_call(
        functools.partial(_mlstm_kernel, nh=nh, unroll=4),
        out_shape=(jax.ShapeDtypeStruct((bsz * seq, w_ml), BF16),
                   jax.ShapeDtypeStruct((bsz * ctx_len, w_ml), BF16)),
        grid=(bsz, nh),
        in_specs=[colspec(seq, 0), colspec(seq, 1), colspec(seq, 2), colspec(seq, 3), colspec(seq, 4),
                  colspec(ctx_len, 0), colspec(ctx_len, 1), colspec(ctx_len, 2), colspec(ctx_len, 3),
                  colspec(ctx_len, 4),
                  gspec(seq), gspec(seq), gspec(seq), gspec(ctx_len), gspec(ctx_len), gspec(ctx_len),
                  pl.BlockSpec((1, dh), lambda b, h: (0, h))],
        out_specs=(pl.BlockSpec((seq, dh), lambda b, h: (b, h)),
                   pl.BlockSpec((ctx_len, dh), lambda b, h: (b, h))),
        scratch_shapes=[pltpu.VMEM((tot, CHUNK), F32),
                        pltpu.VMEM((2, nct, dh, 2 * dh), F32),
                        pltpu.VMEM((2, nct, 8, LANES), F32),
                        pltpu.VMEM((2, nct, dh, 2 * dh), BF16),
                        pltpu.VMEM((2, nct, 8, LANES), F32),
                        pltpu.VMEM((2, dh, 2 * dh), F32)],
        compiler_params=_cparams(("arbitrary", "arbitrary")),
        name="mlstm",
    )(px, px, px, px, px, pc, pc, pc, pc, pc, *gates_x, *gates_c, g_head)


def _glu(a_ref, g_ref, r0, rows):
    a = a_ref[r0:r0 + rows, :].astype(F32)
    return a * _sigmoid(g_ref[r0:r0 + rows, :].astype(F32))


def _segconv_kernel(a_ref, g_ref, w_ref, b_ref, o_ref, pad_ref, *, seg, sub):
    rows, cw = a_ref.shape
    stride = seg + 2 * SEG_PAD
    zeros = jnp.zeros((SEG_PAD, cw), F32)
    for s in range(rows // seg):
        pad_ref[s * stride:s * stride + SEG_PAD, :] = zeros
        pad_ref[s * stride + SEG_PAD:s * stride + SEG_PAD + seg, :] = _glu(a_ref, g_ref, s * seg, seg)
        pad_ref[s * stride + SEG_PAD + seg:(s + 1) * stride, :] = zeros
    for s in range(rows // seg):
        for t0 in range(0, seg, sub):
            start = s * stride + SEG_PAD + t0 - CONV_PAD
            acc = jnp.zeros((sub, cw), F32) + b_ref[...]
            for k in range(CONV_K):
                acc = acc + pad_ref[start + k:start + k + sub, :] * w_ref[k:k + 1, :]
            o_ref[s * seg + t0:s * seg + t0 + sub, :] = acc


def _segconv(p, w_dw, b_dw, *, seg, rows, cw, ncol, a_col0, g_col0, w_col0):
    m = p.shape[0]
    stride = seg + 2 * SEG_PAD
    return pl.pallas_call(
        functools.partial(_segconv_kernel, seg=seg, sub=min(seg, 64)),
        out_shape=jax.ShapeDtypeStruct((m, ncol * cw), F32),
        grid=(m // rows, ncol),
        in_specs=[pl.BlockSpec((rows, cw), lambda i, j: (i, a_col0 + j)),
                  pl.BlockSpec((rows, cw), lambda i, j: (i, g_col0 + j)),
                  pl.BlockSpec((CONV_K, cw), lambda i, j: (0, w_col0 + j)),
                  pl.BlockSpec((1, cw), lambda i, j: (0, w_col0 + j))],
        out_specs=pl.BlockSpec((rows, cw), lambda i, j: (i, j)),
        scratch_shapes=[pltpu.VMEM((rows // seg * stride, cw), F32)],
        compiler_params=_cparams(("arbitrary", "arbitrary")),
        name="segconv",
    )(p, p, w_dw, b_dw)


def _rowconv_kernel(a_ref, g_ref, w_ref, b_ref, o_ref, pad_ref, *, nrows):
    cw = a_ref.shape[1]
    edge = CONV_PAD * GRID_W
    pad_ref[0:edge, :] = jnp.zeros((edge, cw), F32)
    pad_ref[edge + nrows * GRID_W:2 * edge + nrows * GRID_W, :] = jnp.zeros((edge, cw), F32)
    for r in range(nrows):
        pad_ref[edge + r * GRID_W:edge + (r + 1) * GRID_W, :] = _glu(a_ref, g_ref, r * GRID_W, GRID_W)

    def body(r, _):
        acc = jnp.zeros((GRID_W, cw), F32) + b_ref[...]
        for k in range(CONV_K):
            src = pl.multiple_of((r + k) * GRID_W, GRID_W)
            acc = acc + pad_ref[pl.ds(src, GRID_W), :] * w_ref[k:k + 1, :]
        o_ref[pl.ds(pl.multiple_of(r * GRID_W, GRID_W), GRID_W), :] = acc
        return 0
    lax.fori_loop(0, nrows, body, 0)


def _rowconv(p, w_dw, b_dw, *, seq, cw, ncol, a_col0, g_col0, w_col0):
    m = p.shape[0]
    nrows = seq // GRID_W
    return pl.pallas_call(
        functools.partial(_rowconv_kernel, nrows=nrows),
        out_shape=jax.ShapeDtypeStruct((m, ncol * cw), F32),
        grid=(m // seq, ncol),
        in_specs=[pl.BlockSpec((seq, cw), lambda i, j: (i, a_col0 + j)),
                  pl.BlockSpec((seq, cw), lambda i, j: (i, g_col0 + j)),
                  pl.BlockSpec((CONV_K, cw), lambda i, j: (0, w_col0 + j)),
                  pl.BlockSpec((1, cw), lambda i, j: (0, w_col0 + j))],
        out_specs=pl.BlockSpec((seq, cw), lambda i, j: (i, j)),
        scratch_shapes=[pltpu.VMEM((seq + 2 * CONV_PAD * GRID_W, cw), F32)],
        compiler_params=_cparams(("arbitrary", "arbitrary")),
        name="rowconv",
    )(p, p, w_dw, b_dw)


def _out_kernel(cva_ref, cvb_ref, z_ref, ym_ref, x_ref, gt_ref, lng_ref, lnb_ref, gpost_ref,
                wpw32_ref, wout32_ref, o_ref, wpw_ref, wout_ref):
    w_conv = wpw_ref.shape[0]

    @pl.when(pl.program_id(0) == 0)
    def _():
        wpw_ref[...] = wpw32_ref[...].astype(BF16)
        wout_ref[...] = wout32_ref[...].astype(BF16)

    u = jnp.concatenate([cva_ref[...], cvb_ref[...]], axis=1)
    mu = jnp.mean(u, axis=-1, keepdims=True)
    uc = u - mu
    r = uc * lax.rsqrt(jnp.mean(uc * uc, axis=-1, keepdims=True) + EPS) * lng_ref[...] + lnb_ref[...]
    t = jnp.dot((r * _sigmoid(r)).astype(BF16), wpw_ref[...], preferred_element_type=F32)
    z = z_ref[...].astype(F32)
    yc = (t * (z * _sigmoid(z))).astype(BF16)
    out = (jnp.dot(yc, wout_ref[0:w_conv, :], preferred_element_type=F32)
           + jnp.dot(ym_ref[...], wout_ref[w_conv:, :], preferred_element_type=F32))
    ms = jnp.mean(out * out, axis=-1, keepdims=True)
    o_ref[...] = x_ref[...] + gt_ref[0] * (out * lax.rsqrt(ms + EPS) * gpost_ref[...])


def _out_proj(cva, cvb, cvb_col, p, ym, x2d, ada3, row_of_tile, ln_g, ln_b, g_post, w_pw2, w_out,
              *, tm, w_conv, layer):
    m, d = x2d.shape
    half = w_conv // 2
    w_mix = w_out.shape[1]
    once = pl.Buffered(1)
    return pl.pallas_call(
        _out_kernel,
        out_shape=jax.ShapeDtypeStruct((m, d), F32),
        grid=(m // tm,),
        in_specs=[pl.BlockSpec((tm, half), lambda i: (i, 0)),
                  pl.BlockSpec((tm, half), lambda i: (i, cvb_col)),
                  pl.BlockSpec((tm, w_conv), lambda i: (i, 2)),
                  pl.BlockSpec((tm, w_mix - w_conv), lambda i: (i, 0)),
                  pl.BlockSpec((tm, d), lambda i: (i, 0)),
                  pl.BlockSpec((1, 1, d), lambda i: (row_of_tile(i) * 3 + 2, 0, 0)),
                  pl.BlockSpec((1, w_conv), lambda i: (0, 0)),
                  pl.BlockSpec((1, w_conv), lambda i: (0, 0)),
                  pl.BlockSpec((1, d), lambda i: (0, 0)),
                  pl.BlockSpec((None, w_conv, w_conv), lambda i: (layer, 0, 0), pipeline_mode=once),
                  pl.BlockSpec((None, w_mix, d), lambda i: (layer, 0, 0), pipeline_mode=once)],
        out_specs=pl.BlockSpec((tm, d), lambda i: (i, 0)),
        scratch_shapes=[pltpu.VMEM((w_conv, w_conv), BF16), pltpu.VMEM((w_mix, d), BF16)],
        compiler_params=_cparams(("arbitrary",)),
        name="out_proj",
    )(cva, cvb, p, ym, x2d, ada3, ln_g, ln_b, g_post, w_pw2, w_out)


def kernel(x, c, ctx, c_ctx, w_ada, b_ada, g_pre, g_post, w_in, b_gate, w_dw, b_dw, ln_g, ln_b, w_pw2,
           g_head, w_out):
    bsz, seq, d = x.shape
    ctx_len = ctx.shape[1]
    depth = w_ada.shape[0]
    w_conv = w_dw.shape[-1]
    w_ml = g_head.shape[-1]
    nh = b_gate.shape[-1] // 4
    n_main = 3 * w_conv + 5 * w_ml
    half = w_conv // 2
    assert w_ml == nh * LANES and 4 * nh <= LANES and w_conv == w_ml
    assert seq % CHUNK == 0 and ctx_len % CHUNK == 0 and seq % GRID_W == 0 and half % LANES == 0

    ctx_row = bsz
    cond = jnp.concatenate([c, c_ctx[None, :], jnp.zeros((8 - bsz - 1, d), F32)], axis=0)
    ada = _ada_all_layers(cond, w_ada, b_ada)
    ada3 = ada.reshape(depth * 8 * 3, 1, d)

    w_gate =jnp.pad(w_in[:, :, n_main:], ((0, 0), (0, 0), (0, LANES - 4 * nh))).astype(BF16)
    bias_g = jnp.pad(b_gate, ((0, 0), (0, LANES - 4 * nh)))

    tm_x = min(1024, seq)
    tm_c = min(1024, bsz * ctx_len)
    tn = min(1024, w_conv)
    tm_o = min(256, ctx_len)
    cw = min(256, half)

    xs = x.reshape(bsz * seq, d)
    cs = ctx.reshape(bsz * ctx_len, d)
    for l in range(depth):
        last = l == depth - 1
        row_x = lambda i, l=l, t=seq // tm_x: l * 8 + i // t
        row_c = lambda i, l=l: l * 8 + ctx_row + 0 * i
        pc, gc, w_main = _in_proj(cs, ada3, row_c, g_pre[l][None], w_in, w_gate[l],
                                  tm=tm_c, tn=tn // 2, w_conv=w_conv, w_ml=w_ml, layer=l)
        px, gx = _in_proj(xs, ada3, row_x, g_pre[l][None], w_main, w_gate[l],
                          tm=tm_x, tn=tn, w_conv=w_conv, w_ml=w_ml)
        gates_x = _gate_prep(gx, bias_g[l][None], nh, min(1024, seq))
        gates_c = _gate_prep(gc, bias_g[l][None], nh, min(1024, ctx_len))
        ymx, ymc = _mlstm(px, pc, gates_x, gates_c, g_head[l][None], bsz=bsz, seq=seq,
                          ctx_len=ctx_len, nh=nh, w_conv=w_conv, w_ml=w_ml)
        ncol = half // cw
        g0 = w_conv // cw
        cv_w = _segconv(px, w_dw[l], b_dw[l][None], seg=GRID_W, rows=min(512, seq), cw=cw, ncol=ncol,
                        a_col0=0, g_col0=g0, w_col0=0)
        cv_h = _rowconv(px, w_dw[l], b_dw[l][None], seq=seq, cw=cw, ncol=ncol,
                        a_col0=ncol, g_col0=g0 + ncol, w_col0=ncol)
        row_xo = lambda i, l=l, t=seq // tm_o: l * 8 + i // t
        xs_new = _out_proj(cv_w, cv_h, 0, px, ymx, xs, ada3, row_xo, ln_g[l][None], ln_b[l][None],
                           g_post[l][None], w_pw2, w_out, tm=tm_o, w_conv=w_conv, layer=l)
        if not last:
            cv_c = _segconv(pc, w_dw[l], b_dw[l][None], seg=ctx_len, rows=ctx_len, cw=cw,
                            ncol=2 * ncol, a_col0=0, g_col0=g0, w_col0=0)
            cs = _out_proj(cv_c, cv_c, 1, pc, ymc, cs, ada3, row_c, ln_g[l][None], ln_b[l][None],
                           g_post[l][None], w_pw2, w_out, tm=tm_o, w_conv=w_conv, layer=l)
        xs = xs_new
    return xs.reshape(bsz, seq, d)
```

```python
import functools

import jax
import jax.numpy as jnp
from jax import lax
from jax.experimental import pallas as pl
from jax.experimental.pallas import tpu as pltpu

F32 = jnp.float32
BF16 = jnp.bfloat16

GRID_W = 64
CONV_K = 31
CONV_PAD = CONV_K // 2
EPS = 1e-6
NEG = -1e30
LANES = 128
SUBLANES = 8
CHUNK = 128
SEG_PAD = 16
VMEM_LIMIT = 56 * 1024 * 1024


def _cparams(sem):
    return pltpu.CompilerParams(dimension_semantics=sem, vmem_limit_bytes=VMEM_LIMIT)


def _sigmoid(x):
    return 1.0 / (1.0 + jnp.exp(-x))


def _ada_kernel(c_ref, w_ref, b_ref, o_ref):
    c = c_ref[...]
    s = (c * _sigmoid(c)).astype(BF16)
    o_ref[0] = jnp.dot(s, w_ref[0].astype(BF16), preferred_element_type=F32) + b_ref[0]


def _ada_all_layers(cond, w_ada, b_ada):
    depth, d, n = w_ada.shape
    tn = 1024 if n % 1024 == 0 else n // 3
    return pl.pallas_call(
        _ada_kernel,
        out_shape=jax.ShapeDtypeStruct((depth, 8, n), F32),
        grid=(depth, n // tn),
        in_specs=[pl.BlockSpec((8, d), lambda l, j: (0, 0)),
                  pl.BlockSpec((1, d, tn), lambda l, j: (l, 0, j)),
                  pl.BlockSpec((1, 1, tn), lambda l, j: (l, 0, j))],
        out_specs=pl.BlockSpec((1, 8, tn), lambda l, j: (l, 0, j)),
        compiler_params=_cparams(("arbitrary", "arbitrary")),
        name="ada",
    )(cond, w_ada, b_ada.reshape(depth, 1, n))


_NT = (((1,), (1,)), ((), ()))


def _in_kernel(x_ref, sh_ref, sc_ref, gpre_ref, w_ref, wg_ref, p_ref, g_ref, *rest,
               tn, q_lo, q_hi, q_scale, sub):
    hx_ref = rest[-1]
    j = pl.program_id(1)

    @pl.when(j == 0)
    def _():
        tm, d = x_ref.shape
        for r in range(0, tm, sub):
            x = x_ref[r:r + sub, :]
            ms = jnp.mean(x * x, axis=-1, keepdims=True)
            h = (x * lax.rsqrt(ms + EPS) * gpre_ref[...]) * (1.0 + sc_ref[0]) + sh_ref[0]
            hx_ref[r:r + sub, :] = h.astype(BF16)
        wg = wg_ref[...].astype(BF16)
        wg = jnp.concatenate([wg, jnp.zeros((LANES - wg.shape[0], d), BF16)], axis=0)
        g_ref[...] = lax.dot_general(hx_ref[...], wg, _NT, preferred_element_type=F32)

    w = w_ref[...]
    if len(rest) == 2:
        w = w.astype(BF16)
        rest[0][...] = w
    acc = lax.dot_general(hx_ref[...], w, _NT, preferred_element_type=F32)
    col = j * tn
    scale = jnp.where((col >= q_lo) & (col < q_hi), q_scale, 1.0).astype(F32)
    p_ref[...] = (acc * scale).astype(BF16)


def _in_proj(x2d, ada3, row_of_tile, g_pre, w_t, w_in_t, *, tm, tn, w_conv, w_ml, nh, layer,
             emit_w=False):
    m, d = x2d.shape
    n = 3 * w_conv + 5 * w_ml
    q_lo = 3 * w_conv
    kern = functools.partial(_in_kernel, tn=tn, q_lo=q_lo, q_hi=q_lo + w_ml,
                             q_scale=float(LANES) ** -0.5, sub=min(tm, 256))
    out_shape = [jax.ShapeDtypeStruct((m, n), BF16), jax.ShapeDtypeStruct((m, LANES), F32)]
    out_specs = [pl.BlockSpec((tm, tn), lambda i, j: (i, j)),
                 pl.BlockSpec((tm, LANES), lambda i, j: (i, 0))]
    if emit_w:
        assert m == tm
        w_spec = pl.BlockSpec((None, tn, d), lambda i, j: (layer, j, 0))
        out_shape.append(jax.ShapeDtypeStruct((n, d), BF16))
        out_specs.append(pl.BlockSpec((tn, d), lambda i, j: (j, 0)))
    else:
        w_spec = pl.BlockSpec((tn, d), lambda i, j: (j, 0))
    return pl.pallas_call(
        kern,
        out_shape=tuple(out_shape),
        grid=(m // tm, n // tn),
        in_specs=[pl.BlockSpec((tm, d), lambda i, j: (i, 0)),
                  pl.BlockSpec((1, 1, d), lambda i, j: (row_of_tile(i) * 3 + 0, 0, 0)),
                  pl.BlockSpec((1, 1, d), lambda i, j: (row_of_tile(i) * 3 + 1, 0, 0)),
                  pl.BlockSpec((1, d), lambda i, j: (0, 0)),
                  w_spec,
                  pl.BlockSpec((None, 4 * nh, d), lambda i, j: (layer, n // (4 * nh), 0))],
        out_specs=tuple(out_specs),
        scratch_shapes=[pltpu.VMEM((tm, d), BF16)],
        compiler_params=_cparams(("arbitrary", "arbitrary")),
        name="in_proj",
    )(x2d, ada3, ada3, g_pre, w_t, w_in_t)


def _gate_kernel(g_ref, bias_ref, bt_ref, rt_ref, ct_ref, *, nh):
    row = lax.broadcasted_iota(jnp.int32, (CHUNK, CHUNK), 0)
    col = lax.broadcasted_iota(jnp.int32, (CHUNK, CHUNK), 1)
    tril = jnp.where(row >= col, 1.0, 0.0).astype(BF16)
    triu = jnp.where(row <= col, 1.0, 0.0).astype(BF16)
    is_ff = (col >= nh) & (col < 2 * nh)
    is_fb = (col >= 3 * nh) & (col < 4 * nh)
    grow = lax.broadcasted_iota(jnp.int32, (4 * nh, CHUNK), 0)
    tok = lax.broadcasted_iota(jnp.int32, (4 * nh, CHUNK), 1)

    def csum(tri, parts):
        out = jnp.dot(tri, parts[0], preferred_element_type=F32)
        for p in parts[1:]:
            out = out + jnp.dot(tri, p, preferred_element_type=F32)
        return out

    for c in range(g_ref.shape[0] // CHUNK):
        g = g_ref[c * CHUNK:(c + 1) * CHUNK, :] + bias_ref[...]
        lf = jnp.minimum(g, 0.0) - jnp.log(1.0 + jnp.exp(-jnp.abs(g)))
        hi = lf.astype(BF16)
        r1 = lf - hi.astype(F32)
        mid = r1.astype(BF16)
        lo = (r1 - mid.astype(F32)).astype(BF16)
        parts = (hi, mid, lo)
        bmat = jnp.where(is_ff, csum(tril, parts), jnp.where(is_fb, csum(triu, parts), 0.0))
        li = pltpu.roll(g, nh, axis=1)
        rmat = li - bmat
        bt_ref[c] = bmat.T[0:4 * nh, :]
        rt = rmat.T[0:4 * nh, :]
        rt_ref[c] = rt
        cf = rt
        cb = rt
        sh = 1
        while sh < CHUNK:
            cf = jnp.maximum(cf, jnp.where(tok >= sh, pltpu.roll(cf, sh, axis=1), NEG))
            cb = jnp.maximum(cb, jnp.where(tok < CHUNK - sh, pltpu.roll(cb, CHUNK - sh, axis=1), NEG))
            sh *= 2
        ct_ref[c] = jnp.where(grow < 2 * nh, cf, cb)


def _gate_prep(g2d, bias, nh, rows):
    m = g2d.shape[0]
    nc = rows // CHUNK
    out = jax.ShapeDtypeStruct((m // CHUNK, 4 * nh, CHUNK), F32)
    ospec = pl.BlockSpec((nc, 4 * nh, CHUNK), lambda i: (i, 0, 0))
    return pl.pallas_call(
        functools.partial(_gate_kernel, nh=nh),
        out_shape=(out, out, out),
        grid=(m // rows,),
        in_specs=[pl.BlockSpec((rows, LANES), lambda i: (i, 0)),
                  pl.BlockSpec((1, LANES), lambda i: (0, 0))],
        out_specs=(ospec, ospec, ospec),
        compiler_params=_cparams(("arbitrary",)),
        name="gate_prep",
    )(g2d, bias)


def _mlstm_kernel(qx_ref, kx_ref, vx_ref, ox_ref, zx_ref, qc_ref, kc_ref, vc_ref, oc_ref, zc_ref,
                  btx_ref, rtx_ref, ctx_ref, btc_ref, rtc_ref, ctc_ref, gh_ref, yx_ref, yc_ref,
                  s_s, kv_s, sc_s, cp_s, mp_s, st_s, *, nh, unroll):
    h_idx = pl.program_id(1)
    row = lax.broadcasted_iota(jnp.int32, (CHUNK, CHUNK), 0)
    col = lax.broadcasted_iota(jnp.int32, (CHUNK, CHUNK), 1)
    masks = (row >= col, row <= col)
    ones_blk = jnp.ones((CHUNK, LANES), BF16)
    ncx = qx_ref.shape[0] // CHUNK
    ncc = qc_ref.shape[0] // CHUNK
    nct = ncc + ncx
    g_rows = (nh + h_idx, 3 * nh + h_idx)
    last = (CHUNK - 1, 0)

    def loop_a(nc, c0, q_ref, k_ref, v_ref, bt_ref, rt_ref, ct_ref):
        def body(c, _):
            r0 = pl.multiple_of(c * CHUNK, CHUNK)
            rs = pl.multiple_of((c0 + c) * CHUNK, CHUNK)
            kt = k_ref[pl.ds(r0, CHUNK), :].astype(F32).T
            s_s[pl.ds(rs, CHUNK), :] = jnp.dot(q_ref[pl.ds(r0, CHUNK), :], kt.astype(BF16),
                                               preferred_element_type=F32)
            vaug = jnp.concatenate([v_ref[pl.ds(r0, CHUNK), :], ones_blk], axis=1)
            for d in range(2):
                b_last = bt_ref[c, pl.ds(g_rows[d], 1), last[d]:last[d] + 1]
                mg = b_last + ct_ref[c, pl.ds(g_rows[d], 1), last[d]:last[d] + 1]
                gl = b_last + rt_ref[c, pl.ds(g_rows[d], 1), :]
                kws = (kt * jnp.exp(gl - mg)).astype(BF16)
                kv_s[d, c0 + c] = jnp.dot(kws, vaug, preferred_element_type=F32)
                sc_s[d, c0 + c, 0:1, :] = jnp.broadcast_to(b_last, (1, LANES))
                sc_s[d, c0 + c, 1:2, :] = jnp.broadcast_to(mg, (1, LANES))
            return 0
        lax.fori_loop(0, nc, body, 0, unroll=unroll)

    loop_a(ncc, 0, qc_ref, kc_ref, vc_ref, btc_ref, rtc_ref, ctc_ref)
    loop_a(ncx, ncc, qx_ref, kx_ref, vx_ref, btx_ref, rtx_ref, ctx_ref)

    st_s[...] = jnp.zeros_like(st_s)

    def scan(j, ms):
        new = []
        for d in range(2):
            g = j if d == 0 else jnp.where(j < ncc, ncc - 1 - j, nct - 1 - j + ncc)
            m_prev = ms[d]
            c_prev = st_s[d]
            cp_s[d, g] = c_prev.astype(BF16)
            mp_s[d, g] = jnp.broadcast_to(m_prev, (8, LANES))
            b_last = sc_s[d, g, 0:1, :]
            mg = sc_s[d, g, 1:2, :]
            m_new = jnp.maximum(b_last + m_prev, mg)
            a = jnp.exp(b_last + m_prev - m_new)
            w = jnp.exp(mg - m_new)
            st_s[d] = (jnp.concatenate([a, a], axis=1) * c_prev
                       + jnp.concatenate([w, w], axis=1) * kv_s[d, g])
            new.append(m_new)
        return tuple(new)

    m0 = jnp.full((1, LANES), NEG, F32)
    lax.fori_loop(0, nct, scan, (m0, m0))

    def loop_b(nc, c0, q_ref, v_ref, o_ref, z_ref, bt_ref, rt_ref, ct_ref, y_ref):
        def body(c, _):
            r0 = pl.multiple_of(c * CHUNK, CHUNK)
            rs = pl.multiple_of((c0 + c) * CHUNK, CHUNK)
            q = q_ref[pl.ds(r0, CHUNK), :].astype(F32)
            s = s_s[pl.ds(rs, CHUNK), :]
            vaug = jnp.concatenate([v_ref[pl.ds(r0, CHUNK), :], ones_blk], axis=1)
            h = None
            for d in range(2):
                c_row = ct_ref[c, pl.ds(g_rows[d], 1), :]
                r_row = rt_ref[c, pl.ds(g_rows[d], 1), :]
                m_prev = mp_s[d, c0 + c, 0:1, :]
                floor_bc = jnp.broadcast_to(
                    jnp.exp(-(bt_ref[c, pl.ds(g_rows[d], 1), :] + jnp.maximum(c_row, m_prev))),
                    (CHUNK, CHUNK)).T
                cm_bc = jnp.broadcast_to(c_row, (CHUNK, CHUNK)).T
                mm = jnp.maximum(cm_bc, m_prev)
                qw = (q * jnp.exp(m_prev - mm)).astype(BF16)
                pm = (jnp.exp(jnp.where(masks[d], r_row - mm, NEG)) * s).astype(BF16)
                nd = jnp.dot(jnp.concatenate([qw, pm], axis=1),
                             jnp.concatenate([cp_s[d, c0 + c], vaug], axis=0),
                             preferred_element_type=F32)
                hd = nd[:, 0:LANES] / jnp.maximum(jnp.abs(nd[:, LANES:]), floor_bc)
                h = hd if h is None else h + hd
            mu = jnp.mean(h, axis=1, keepdims=True)
            hc = h - mu
            hn = hc * lax.rsqrt(jnp.mean(hc * hc, axis=1, keepdims=True) + EPS) * gh_ref[...]
            z = z_ref[pl.ds(r0, CHUNK), :].astype(F32)
            o = o_ref[pl.ds(r0, CHUNK), :].astype(F32)
            y = hn * (z / ((1.0 + jnp.exp(-o)) * (1.0 + jnp.exp(-z))))
            y_ref[pl.ds(r0, CHUNK), :] = y.astype(BF16)
            return 0
        lax.fori_loop(0, nc, body, 0, unroll=unroll)

    loop_b(ncc, 0, qc_ref, vc_ref, oc_ref, zc_ref, btc_ref, rtc_ref, ctc_ref, yc_ref)
    loop_b(ncx, ncc, qx_ref, vx_ref, ox_ref, zx_ref, btx_ref, rtx_ref, ctx_ref, yx_ref)


def _mlstm(px, pc, gates_x, gates_c, g_head, *, bsz, seq, ctx_len, nh, w_conv, w_ml):
    dh = LANES
    cb = 3 * w_conv // dh
    hb = w_ml // dh

    def colspec(rows, k):
        return pl.BlockSpec((rows, dh), lambda b, h: (b, cb + k * hb + h))

    def gspec(rows):
        return pl.BlockSpec((rows // CHUNK, 4 * nh, CHUNK), lambda b, h: (b, 0, 0))

    tot = seq + ctx_len
    nct = tot // CHUNK
    return pl.pallas_call(
        functools.partial(_mlstm_kernel, nh=nh, unroll=4),
        out_shape=(jax.ShapeDtypeStruct((bsz * seq, w_ml), BF16),
                   jax.ShapeDtypeStruct((bsz * ctx_len, w_ml), BF16)),
        grid=(bsz, nh),
        in_specs=[colspec(seq, 0), colspec(seq, 1), colspec(seq, 2), colspec(seq, 3), colspec(seq, 4),
                  colspec(ctx_len, 0), colspec(ctx_len, 1), colspec(ctx_len, 2), colspec(ctx_len, 3),
                  colspec(ctx_len, 4),
                  gspec(seq), gspec(seq), gspec(seq), gspec(ctx_len), gspec(ctx_len), gspec(ctx_len),
                  pl.BlockSpec((1, dh), lambda b, h: (0, h))],
        out_specs=(pl.BlockSpec((seq, dh), lambda b, h: (b, h)),
                   pl.BlockSpec((ctx_len, dh), lambda b, h: (b, h))),
        scratch_shapes=[pltpu.VMEM((tot, CHUNK), F32),
                        pltpu.VMEM((2, nct, dh, 2 * dh), F32),
                        pltpu.VMEM((2, nct, 8, LANES), F32),
                        pltpu.VMEM((2, nct, dh, 2 * dh), BF16),
                        pltpu.VMEM((2, nct, 8, LANES), F32),
                        pltpu.VMEM((2, dh, 2 * dh), F32)],
        compiler_params=_cparams(("arbitrary", "arbitrary")),
        name="mlstm",
    )(px, px, px, px, px, pc, pc, pc, pc, pc, *gates_x, *gates_c, g_head)


def _glu(a_ref, g_ref, r0, rows):
    a = a_ref[r0:r0 + rows, :].astype(F32)
    return a * _sigmoid(g_ref[r0:r0 + rows, :].astype(F32))


def _segconv_kernel(a_ref, g_ref, w_ref, b_ref, o_ref, pad_ref, *, seg, sub):
    rows, cw = a_ref.shape
    stride = seg + 2 * SEG_PAD
    total = rows // seg * stride
    zeros = jnp.zeros((SEG_PAD, cw), F32)
    for s in range(rows // seg):
        pad_ref[0, s * stride:s * stride + SEG_PAD, :] = zeros
        pad_ref[0, s * stride + SEG_PAD:s * stride + SEG_PAD + seg, :] = _glu(a_ref, g_ref, s * seg, seg)
        pad_ref[0, s * stride + SEG_PAD + seg:(s + 1) * stride, :] = zeros
    pad_ref[0, total:total + SUBLANES, :] = jnp.zeros((SUBLANES, cw), F32)
    for r in range(1, SUBLANES):
        pad_ref[r, 0:total, :] = pad_ref[0, r:r + total, :]
    for s in range(rows // seg):
        for t0 in range(0, seg, sub):
            start = s * stride + SEG_PAD + t0 - CONV_PAD
            acc = jnp.zeros((sub, cw), F32) + b_ref[...]
            for k in range(CONV_K):
                r = (start + k) % SUBLANES
                acc = acc + pad_ref[r, start + k - r:start + k - r + sub, :] * w_ref[k:k + 1, :]
            o_ref[s * seg + t0:s * seg + t0 + sub, :] = acc


def _segconv(p, w_dw, b_dw, *, seg, rows, cw, ncol, a_col0, g_col0, w_col0):
    m = p.shape[0]
    stride = seg + 2 * SEG_PAD
    return pl.pallas_call(
        functools.partial(_segconv_kernel, seg=seg, sub=min(seg, 64)),
        out_shape=jax.ShapeDtypeStruct((m, ncol * cw), F32),
        grid=(m // rows, ncol),
        in_specs=[pl.BlockSpec((rows, cw), lambda i, j: (i, a_col0 + j)),
                  pl.BlockSpec((rows, cw), lambda i, j: (i, g_col0 + j)),
                  pl.BlockSpec((CONV_K, cw), lambda i, j: (0, w_col0 + j)),
                  pl.BlockSpec((1, cw), lambda i, j: (0, w_col0 + j))],
        out_specs=pl.BlockSpec((rows, cw), lambda i, j: (i, j)),
        scratch_shapes=[pltpu.VMEM((SUBLANES, rows // seg * stride + SUBLANES, cw), F32)],
        compiler_params=_cparams(("arbitrary", "arbitrary")),
        name="segconv",
    )(p, p, w_dw, b_dw)


def _rowconv_kernel(a_ref, g_ref, w_ref, b_ref, o_ref, pad_ref, *, nrows):
    cw = a_ref.shape[1]
    edge = CONV_PAD * GRID_W
    pad_ref[0:edge, :] = jnp.zeros((edge, cw), F32)
    pad_ref[edge + nrows * GRID_W:2 * edge + nrows * GRID_W, :] = jnp.zeros((edge, cw), F32)
    for r in range(nrows):
        pad_ref[edge + r * GRID_W:edge + (r + 1) * GRID_W, :] = _glu(a_ref, g_ref, r * GRID_W, GRID_W)

    def body(r, _):
        acc = jnp.zeros((GRID_W, cw), F32) + b_ref[...]
        for k in range(CONV_K):
            src = pl.multiple_of((r + k) * GRID_W, GRID_W)
            acc = acc + pad_ref[pl.ds(src, GRID_W), :] * w_ref[k:k + 1, :]
        o_ref[pl.ds(pl.multiple_of(r * GRID_W, GRID_W), GRID_W), :] = acc
        return 0
    lax.fori_loop(0, nrows, body, 0)


def _rowconv(p, w_dw, b_dw, *, seq, cw, ncol, a_col0, g_col0, w_col0):
    m = p.shape[0]
    nrows = seq // GRID_W
    return pl.pallas_call(
        functools.partial(_rowconv_kernel, nrows=nrows),
        out_shape=jax.ShapeDtypeStruct((m, ncol * cw), F32),
        grid=(m // seq, ncol),
        in_specs=[pl.BlockSpec((seq, cw), lambda i, j: (i, a_col0 + j)),
                  pl.BlockSpec((seq, cw), lambda i, j: (i, g_col0 + j)),
                  pl.BlockSpec((CONV_K, cw), lambda i, j: (0, w_col0 + j)),
                  pl.BlockSpec((1, cw), lambda i, j: (0, w_col0 + j))],
        out_specs=pl.BlockSpec((seq, cw), lambda i, j: (i, j)),
        scratch_shapes=[pltpu.VMEM((seq + 2 * CONV_PAD * GRID_W, cw), F32)],
        compiler_params=_cparams(("arbitrary", "arbitrary")),
        name="rowconv",
    )(p, p, w_dw, b_dw)


def _out_kernel(cva_ref, cvb_ref, z_ref, ym_ref, x_ref, gt_ref, lng_ref, lnb_ref, gpost_ref,
                wpw_ref, wout_ref, o_ref):
    w_conv = wpw_ref.shape[0]
    u = jnp.concatenate([cva_ref[...], cvb_ref[...]], axis=1)
    mu = jnp.mean(u, axis=-1, keepdims=True)
    uc = u - mu
    r = uc * lax.rsqrt(jnp.mean(uc * uc, axis=-1, keepdims=True) + EPS) * lng_ref[...] + lnb_ref[...]
    t = jnp.dot((r * _sigmoid(r)).astype(BF16), wpw_ref[...], preferred_element_type=F32)
    z = z_ref[...].astype(F32)
    yc = (t * (z * _sigmoid(z))).astype(BF16)
    out = (jnp.dot(yc, wout_ref[0:w_conv, :], preferred_element_type=F32)
           + jnp.dot(ym_ref[...], wout_ref[w_conv:, :], preferred_element_type=F32))
    ms = jnp.mean(out * out, axis=-1, keepdims=True)
    o_ref[...] = x_ref[...] + gt_ref[0] * (out * lax.rsqrt(ms + EPS) * gpost_ref[...])


def _out_proj(cva, cvb, cvb_col, p, ym, x2d, ada3, row_of_tile, ln_g, ln_b, g_post, w_pw2, w_out,
              *, tm, w_conv, layer):
    m, d = x2d.shape
    half = w_conv // 2
    w_mix = w_out.shape[1]
    once = pl.Buffered(1)
    return pl.pallas_call(
        _out_kernel,
        out_shape=jax.ShapeDtypeStruct((m, d), F32),
        grid=(m // tm,),
        in_specs=[pl.BlockSpec((tm, half), lambda i: (i, 0)),
                  pl.BlockSpec((tm, half), lambda i: (i, cvb_col)),
                  pl.BlockSpec((tm, w_conv), lambda i: (i, 2)),
                  pl.BlockSpec((tm, w_mix - w_conv), lambda i: (i, 0)),
                  pl.BlockSpec((tm, d), lambda i: (i, 0)),
                  pl.BlockSpec((1, 1, d), lambda i: (row_of_tile(i) * 3 + 2, 0, 0)),
                  pl.BlockSpec((1, w_conv), lambda i: (0, 0)),
                  pl.BlockSpec((1, w_conv), lambda i: (0, 0)),
                  pl.BlockSpec((1, d), lambda i: (0, 0)),
                  pl.BlockSpec((None, w_conv, w_conv), lambda i: (layer, 0, 0), pipeline_mode=once),
                  pl.BlockSpec((None, w_mix, d), lambda i: (layer, 0, 0), pipeline_mode=once)],
        out_specs=pl.BlockSpec((tm, d), lambda i: (i, 0)),
        compiler_params=_cparams(("arbitrary",)),
        name="out_proj",
    )(cva, cvb, p, ym, x2d, ada3, ln_g, ln_b, g_post, w_pw2, w_out)


def kernel(x, c, ctx, c_ctx, w_ada, b_ada, g_pre, g_post, w_in, b_gate, w_dw, b_dw, ln_g, ln_b, w_pw2,
           g_head, w_out):
    bsz, seq, d = x.shape
    ctx_len = ctx.shape[1]
    depth = w_ada.shape[0]
    w_conv = w_dw.shape[-1]
    w_ml = g_head.shape[-1]
    nh = b_gate.shape[-1] // 4
    n_main = 3 * w_conv + 5 * w_ml
    half = w_conv // 2
    assert w_ml == nh * LANES and 4 * nh <= LANES and w_conv == w_ml
    assert seq % CHUNK == 0 and ctx_len % CHUNK == 0 and seq % GRID_W == 0 and half % LANES == 0

    ctx_row = bsz
    cond = jnp.concatenate([c, c_ctx[None, :], jnp.zeros((8 - bsz - 1, d), F32)], axis=0)
    ada = _ada_all_layers(cond, w_ada, b_ada)
    ada3 = ada.reshape(depth * 8 * 3, 1, d)

    w_in_t = jnp.swapaxes(w_in, 1, 2)
    bias_g = jnp.pad(b_gate, ((0, 0), (0, LANES - 4 * nh)))

    tm_x = min(1024, seq)
    tm_c = min(1024, bsz * ctx_len)
    tn = min(1024, w_conv)
    tm_o = min(512, seq)
    tm_oc = min(256, ctx_len)
    cw = min(256, half)
    w_pw2_b = w_pw2.astype(BF16)
    w_out_b = w_out.astype(BF16)

    xs = x.reshape(bsz * seq, d)
    cs = ctx.reshape(bsz * ctx_len, d)
    for l in range(depth):
        last = l == depth - 1
        row_x = lambda i, l=l, t=seq // tm_x: l * 8 + i // t
        row_c = lambda i, l=l: l * 8 + ctx_row + 0 * i
        pc, gc, w_main_t = _in_proj(cs, ada3, row_c, g_pre[l][None], w_in_t, w_in_t, tm=tm_c,
                                    tn=tn // 2, w_conv=w_conv, w_ml=w_ml, nh=nh, layer=l, emit_w=True)
        px, gx = _in_proj(xs, ada3, row_x, g_pre[l][None], w_main_t, w_in_t,
                          tm=tm_x, tn=tn, w_conv=w_conv, w_ml=w_ml, nh=nh, layer=l)
        gates_x = _gate_prep(gx, bias_g[l][None], nh, min(1024, seq))
        gates_c = _gate_prep(gc, bias_g[l][None], nh, min(1024, ctx_len))
        ymx, ymc = _mlstm(px, pc, gates_x, gates_c, g_head[l][None], bsz=bsz, seq=seq,
                          ctx_len=ctx_len, nh=nh, w_conv=w_conv, w_ml=w_ml)
        ncol = half // cw
        g0 = w_conv // cw
        cv_w = _segconv(px, w_dw[l], b_dw[l][None], seg=GRID_W, rows=min(512, seq), cw=cw, ncol=ncol,
                        a_col0=0, g_col0=g0, w_col0=0)
        cv_h = _rowconv(px, w_dw[l], b_dw[l][None], seq=seq, cw=cw, ncol=ncol,
                        a_col0=ncol, g_col0=g0 + ncol, w_col0=ncol)
        row_xo = lambda i, l=l, t=seq // tm_o: l * 8 + i // t
        xs_new = _out_proj(cv_w, cv_h, 0, px, ymx, xs, ada3, row_xo, ln_g[l][None], ln_b[l][None],
                           g_post[l][None], w_pw2_b, w_out_b, tm=tm_o, w_conv=w_conv, layer=l)
        if not last:
            cv_c = _segconv(pc, w_dw[l], b_dw[l][None], seg=ctx_len, rows=ctx_len, cw=cw,
                            ncol=2 * ncol, a_col0=0, g_col0=g0, w_col0=0)
            cs = _out_proj(cv_c, cv_c, 1, pc, ymc, cs, ada3, row_c, ln_g[l][None], ln_b[l][None],
                           g_post[l][None], w_pw2_b, w_out_b, tm=tm_oc, w_conv=w_conv, layer=l)
        xs = xs_new
    return xs.reshape(bsz, seq, d)
```

```python
import functools

import jax
import jax.numpy as jnp
from jax import lax
from jax.experimental import pallas as pl
from jax.experimental.pallas import tpu as pltpu

F32 = jnp.float32
BF16 = jnp.bfloat16

GRID_W = 64
CONV_K = 31
CONV_PAD = CONV_K // 2
EPS = 1e-6
NEG = -1e30
LOG2E = 1.4426950408889634
LANES = 128
SUBLANES = 8
CHUNK = 128
SEG_PAD = 16
VMEM_LIMIT = 56 * 1024 * 1024


def _cparams(sem):
    return pltpu.CompilerParams(dimension_semantics=sem, vmem_limit_bytes=VMEM_LIMIT)


def _sigmoid(x):
    return 1.0 / (1.0 + jnp.exp(-x))


def _ada_kernel(c_ref, w_ref, b_ref, o_ref):
    c = c_ref[...]
    s = (c * _sigmoid(c)).astype(BF16)
    o_ref[0] = jnp.dot(s, w_ref[0].astype(BF16), preferred_element_type=F32) + b_ref[0]


def _ada_all_layers(cond, w_ada, b_ada):
    depth, d, n = w_ada.shape
    tn = 1024 if n % 1024 == 0 else n // 3
    return pl.pallas_call(
        _ada_kernel,
        out_shape=jax.ShapeDtypeStruct((depth, 8, n), F32),
        grid=(depth, n // tn),
        in_specs=[pl.BlockSpec((8, d), lambda l, j: (0, 0)),
                  pl.BlockSpec((1, d, tn), lambda l, j: (l, 0, j)),
                  pl.BlockSpec((1, 1, tn), lambda l, j: (l, 0, j))],
        out_specs=pl.BlockSpec((1, 8, tn), lambda l, j: (l, 0, j)),
        compiler_params=_cparams(("arbitrary", "arbitrary")),
        name="ada",
    )(cond, w_ada, b_ada.reshape(depth, 1, n))


_NT = (((1,), (1,)), ((), ()))


def _in_kernel(x_ref, sh_ref, sc_ref, gpre_ref, w_ref, wg_ref, p_ref, g_ref, *rest,
               tn, q_lo, q_hi, q_scale, sub):
    hx_ref = rest[-1]
    j = pl.program_id(1)

    @pl.when(j == 0)
    def _():
        tm, d = x_ref.shape
        for r in range(0, tm, sub):
            x = x_ref[r:r + sub, :]
            ms = jnp.mean(x * x, axis=-1, keepdims=True)
            h = (x * lax.rsqrt(ms + EPS) * gpre_ref[...]) * (1.0 + sc_ref[0]) + sh_ref[0]
            hx_ref[r:r + sub, :] = h.astype(BF16)
        wg = wg_ref[...].astype(BF16)
        wg = jnp.concatenate([wg, jnp.zeros((LANES - wg.shape[0], d), BF16)], axis=0)
        g_ref[...] = lax.dot_general(hx_ref[...], wg, _NT, preferred_element_type=F32)

    w = w_ref[...]
    if len(rest) == 2:
        w = w.astype(BF16)
        rest[0][...] = w
    acc = lax.dot_general(hx_ref[...], w, _NT, preferred_element_type=F32)
    col = j * tn
    scale = jnp.where((col >= q_lo) & (col < q_hi), q_scale, 1.0).astype(F32)
    p_ref[...] = (acc * scale).astype(BF16)


def _in_proj(x2d, ada3, row_of_tile, g_pre, w_t, w_in_t, *, tm, tn, w_conv, w_ml, nh, layer,
             emit_w=False):
    m, d = x2d.shape
    n = 3 * w_conv + 5 * w_ml
    q_lo = 3 * w_conv
    kern = functools.partial(_in_kernel, tn=tn, q_lo=q_lo, q_hi=q_lo + w_ml,
                             q_scale=float(LANES) ** -0.5, sub=min(tm, 256))
    out_shape = [jax.ShapeDtypeStruct((m, n), BF16), jax.ShapeDtypeStruct((m, LANES), F32)]
    out_specs = [pl.BlockSpec((tm, tn), lambda i, j: (i, j)),
                 pl.BlockSpec((tm, LANES), lambda i, j: (i, 0))]
    if emit_w:
        assert m == tm
        w_spec = pl.BlockSpec((None, tn, d), lambda i, j: (layer, j, 0))
        out_shape.append(jax.ShapeDtypeStruct((n, d), BF16))
        out_specs.append(pl.BlockSpec((tn, d), lambda i, j: (j, 0)))
    else:
        w_spec = pl.BlockSpec((tn, d), lambda i, j: (j, 0))
    return pl.pallas_call(
        kern,
        out_shape=tuple(out_shape),
        grid=(m // tm, n // tn),
        in_specs=[pl.BlockSpec((tm, d), lambda i, j: (i, 0)),
                  pl.BlockSpec((1, 1, d), lambda i, j: (row_of_tile(i) * 3 + 0, 0, 0)),
                  pl.BlockSpec((1, 1, d), lambda i, j: (row_of_tile(i) * 3 + 1, 0, 0)),
                  pl.BlockSpec((1, d), lambda i, j: (0, 0)),
                  w_spec,
                  pl.BlockSpec((None, 4 * nh, d), lambda i, j: (layer, n // (4 * nh), 0))],
        out_specs=tuple(out_specs),
        scratch_shapes=[pltpu.VMEM((tm, d), BF16)],
        compiler_params=_cparams(("arbitrary", "arbitrary")),
        name="in_proj",
    )(x2d, ada3, ada3, g_pre, w_t, w_in_t)


def _gate_kernel(g_ref, bias_ref, bt_ref, rt_ref, ct_ref, *, nh):
    row = lax.broadcasted_iota(jnp.int32, (CHUNK, CHUNK), 0)
    col = lax.broadcasted_iota(jnp.int32, (CHUNK, CHUNK), 1)
    tril = jnp.where(row >= col, 1.0, 0.0).astype(BF16)
    triu = jnp.where(row <= col, 1.0, 0.0).astype(BF16)
    is_ff = (col >= nh) & (col < 2 * nh)
    is_fb = (col >= 3 * nh) & (col < 4 * nh)
    grow = lax.broadcasted_iota(jnp.int32, (4 * nh, CHUNK), 0)
    tok = lax.broadcasted_iota(jnp.int32, (4 * nh, CHUNK), 1)

    def csum(tri, parts):
        out = jnp.dot(tri, parts[0], preferred_element_type=F32)
        for p in parts[1:]:
            out = out + jnp.dot(tri, p, preferred_element_type=F32)
        return out

    for c in range(g_ref.shape[0] // CHUNK):
        g = g_ref[c * CHUNK:(c + 1) * CHUNK, :] + bias_ref[...]
        lf = jnp.minimum(g, 0.0) - jnp.log(1.0 + jnp.exp(-jnp.abs(g)))
        hi = lf.astype(BF16)
        r1 = lf - hi.astype(F32)
        mid = r1.astype(BF16)
        lo = (r1 - mid.astype(F32)).astype(BF16)
        parts = (hi, mid, lo)
        bmat = jnp.where(is_ff, csum(tril, parts), jnp.where(is_fb, csum(triu, parts), 0.0))
        li = pltpu.roll(g, nh, axis=1)
        rmat = li - bmat
        bt_ref[c] = bmat.T[0:4 * nh, :] * LOG2E
        rt = rmat.T[0:4 * nh, :] * LOG2E
        rt_ref[c] = rt
        cf = rt
        cb = rt
        sh = 1
        while sh < CHUNK:
            cf = jnp.maximum(cf, jnp.where(tok >= sh, pltpu.roll(cf, sh, axis=1), NEG))
            cb = jnp.maximum(cb, jnp.where(tok < CHUNK - sh, pltpu.roll(cb, CHUNK - sh, axis=1), NEG))
            sh *= 2
        ct_ref[c] = jnp.where(grow < 2 * nh, cf, cb).astype(BF16).astype(F32)


def _gate_prep(g2d, bias, nh, rows):
    m = g2d.shape[0]
    nc = rows // CHUNK
    out = jax.ShapeDtypeStruct((m // CHUNK, 4 * nh, CHUNK), F32)
    ospec = pl.BlockSpec((nc, 4 * nh, CHUNK), lambda i: (i, 0, 0))
    return pl.pallas_call(
        functools.partial(_gate_kernel, nh=nh),
        out_shape=(out, out, out),
        grid=(m // rows,),
        in_specs=[pl.BlockSpec((rows, LANES), lambda i: (i, 0)),
                  pl.BlockSpec((1, LANES), lambda i: (0, 0))],
        out_specs=(ospec, ospec, ospec),
        compiler_params=_cparams(("arbitrary",)),
        name="gate_prep",
    )(g2d, bias)


def _mlstm_kernel(qx_ref, kx_ref, vx_ref, ox_ref, zx_ref, qc_ref, kc_ref, vc_ref, oc_ref, zc_ref,
                  btx_ref, rtx_ref, ctx_ref, btc_ref, rtc_ref, ctc_ref, gh_ref, yx_ref, yc_ref,
                  s_s, kv_s, sc_s, cp_s, mp_s, *, nh, unroll):
    h_idx = pl.program_id(1)
    row = lax.broadcasted_iota(jnp.int32, (CHUNK, CHUNK), 0)
    col = lax.broadcasted_iota(jnp.int32, (CHUNK, CHUNK), 1)
    masks = (row >= col, row <= col)
    ones_blk = jnp.ones((CHUNK, LANES), BF16)
    eye_b = jnp.where(row == col, 1.0, 0.0).astype(BF16)
    row2 = lax.broadcasted_iota(jnp.int32, (2 * CHUNK, 2 * LANES), 0)
    col2 = lax.broadcasted_iota(jnp.int32, (2 * CHUNK, 2 * LANES), 1)
    ones2 = jnp.where((row2 < CHUNK) == (col2 < LANES), 1.0, 0.0).astype(BF16)
    ncx = qx_ref.shape[0] // CHUNK
    ncc = qc_ref.shape[0] // CHUNK
    nct = ncc + ncx
    g_rows = (nh + h_idx, 3 * nh + h_idx)
    last = (CHUNK - 1, 0)

    def loop_a(nc, c0, q_ref, k_ref, v_ref, bt_ref, rt_ref, ct_ref):
        def body(c, _):
            r0 = pl.multiple_of(c * CHUNK, CHUNK)
            rs = pl.multiple_of((c0 + c) * CHUNK, CHUNK)
            kt = k_ref[pl.ds(r0, CHUNK), :].astype(F32).T
            s_s[pl.ds(rs, CHUNK), :] = jnp.dot(q_ref[pl.ds(r0, CHUNK), :], kt.astype(BF16),
                                               preferred_element_type=F32)
            vaug = jnp.concatenate([v_ref[pl.ds(r0, CHUNK), :], ones_blk], axis=1)
            for d in range(2):
                b_last = bt_ref[c, pl.ds(g_rows[d], 1), last[d]:last[d] + 1]
                mg = b_last + ct_ref[c, pl.ds(g_rows[d], 1), last[d]:last[d] + 1]
                gl = b_last + rt_ref[c, pl.ds(g_rows[d], 1), :]
                kws = (kt * jnp.exp2(gl - mg)).astype(BF16)
                kv_s[d, c0 + c] = jnp.dot(kws, vaug, preferred_element_type=F32)
                sc_s[d, c0 + c, 0:1, :] = jnp.broadcast_to(b_last, (1, LANES))
                sc_s[d, c0 + c, 1:2, :] = jnp.broadcast_to(mg, (1, LANES))
            return 0
        lax.fori_loop(0, nc, body, 0, unroll=2 * unroll)

    loop_a(ncc, 0, qc_ref, kc_ref, vc_ref, btc_ref, rtc_ref, ctc_ref)
    loop_a(ncx, ncc, qx_ref, kx_ref, vx_ref, btx_ref, rtx_ref, ctx_ref)

    def scan(d):
        def body(j, carry):
            m_prev, c_prev = carry
            g = j if d == 0 else jnp.where(j < ncc, ncc - 1 - j, nct - 1 - j + ncc)
            cp_s[d, g] = c_prev.astype(BF16)
            mp_s[d, g] = jnp.broadcast_to(m_prev, (8, LANES))
            b_last = sc_s[d, g, 0:1, :]
            mg = sc_s[d, g, 1:2, :]
            m_new = jnp.maximum(b_last + m_prev, mg)
            a = jnp.exp2(b_last + m_prev - m_new)
            w = jnp.exp2(mg - m_new)
            c_new = (jnp.concatenate([a, a], axis=1) * c_prev
                     + jnp.concatenate([w, w], axis=1) * kv_s[d, g])
            return (m_new, c_new)
        lax.fori_loop(0, nct, body, (jnp.full((1, LANES), NEG, F32),
                                     jnp.zeros((LANES, 2 * LANES), F32)), unroll=2)

    scan(0)
    scan(1)

    def loop_b(nc, c0, q_ref, v_ref, o_ref, z_ref, bt_ref, rt_ref, ct_ref, y_ref):
        def body(c, _):
            r0 = pl.multiple_of(c * CHUNK, CHUNK)
            rs = pl.multiple_of((c0 + c) * CHUNK, CHUNK)
            q = q_ref[pl.ds(r0, CHUNK), :].astype(F32)
            s = s_s[pl.ds(rs, CHUNK), :]
            vaug = jnp.concatenate([v_ref[pl.ds(r0, CHUNK), :], ones_blk], axis=1)
            h = None
            for d in range(2):
                c_row = ct_ref[c, pl.ds(g_rows[d], 1), :]
                r_row = rt_ref[c, pl.ds(g_rows[d], 1), :]
                m_prev = mp_s[d, c0 + c, 0:1, :]
                floor_row = jnp.exp2(-(bt_ref[c, pl.ds(g_rows[d], 1), :] + jnp.maximum(c_row, m_prev)))
                diag = jnp.concatenate(
                    [jnp.broadcast_to(c_row.astype(BF16), (CHUNK, CHUNK)) * eye_b,
                     jnp.broadcast_to(floor_row.astype(BF16), (CHUNK, CHUNK)) * eye_b], axis=1)
                cols = jnp.dot(diag, ones2, preferred_element_type=F32)
                cm_bc = cols[:, 0:LANES]
                floor_bc = cols[:, LANES:]
                mm = jnp.maximum(cm_bc, m_prev)
                qw = (q * jnp.exp2(m_prev - mm)).astype(BF16)
                pm = (jnp.exp2(jnp.where(masks[d], r_row - mm, NEG)) * s).astype(BF16)
                nd = jnp.dot(jnp.concatenate([qw, pm], axis=1),
                             jnp.concatenate([cp_s[d, c0 + c], vaug], axis=0),
                             preferred_element_type=F32)
                hd = nd[:, 0:LANES] / jnp.maximum(jnp.abs(nd[:, LANES:]), floor_bc)
                h = hd if h is None else h + hd
            mu = jnp.mean(h, axis=1, keepdims=True)
            hc = h - mu
            hn = hc * lax.rsqrt(jnp.mean(hc * hc, axis=1, keepdims=True) + EPS) * gh_ref[...]
            z = z_ref[pl.ds(r0, CHUNK), :].astype(F32)
            o = o_ref[pl.ds(r0, CHUNK), :].astype(F32)
            y = hn * (z / ((1.0 + jnp.exp(-o)) * (1.0 + jnp.exp(-z))))
            y_ref[pl.ds(r0, CHUNK), :] = y.astype(BF16)
            return 0
        lax.fori_loop(0, nc, body, 0, unroll=2 * unroll)

    loop_b(ncc, 0, qc_ref, vc_ref, oc_ref, zc_ref, btc_ref, rtc_ref, ctc_ref, yc_ref)
    loop_b(ncx, ncc, qx_ref, vx_ref, ox_ref, zx_ref, btx_ref, rtx_ref, ctx_ref, yx_ref)


def _mlstm(px, pc, gates_x, gates_c, g_head, *, bsz, seq, ctx_len, nh, w_conv, w_ml):
    dh = LANES
    cb = 3 * w_conv // dh
    hb = w_ml // dh

    def colspec(rows, k):
        return pl.BlockSpec((rows, dh), lambda b, h: (b, cb + k * hb + h))

    def gspec(rows):
        return pl.BlockSpec((rows // CHUNK, 4 * nh, CHUNK), lambda b, h: (b, 0, 0))

    tot = seq + ctx_len
    nct = tot // CHUNK
    return pl.pallas_call(
        functools.partial(_mlstm_kernel, nh=nh, unroll=4),
        out_shape=(jax.ShapeDtypeStruct((bsz * seq, w_ml), BF16),
                   jax.ShapeDtypeStruct((bsz * ctx_len, w_ml), BF16)),
        grid=(bsz, nh),
        in_specs=[colspec(seq, 0), colspec(seq, 1), colspec(seq, 2), colspec(seq, 3), colspec(seq, 4),
                  colspec(ctx_len, 0), colspec(ctx_len, 1), colspec(ctx_len, 2), colspec(ctx_len, 3),
                  colspec(ctx_len, 4),
                  gspec(seq), gspec(seq), gspec(seq), gspec(ctx_len), gspec(ctx_len), gspec(ctx_len),
                  pl.BlockSpec((1, dh), lambda b, h: (0, h))],
        out_specs=(pl.BlockSpec((seq, dh), lambda b, h: (b, h)),
                   pl.BlockSpec((ctx_len, dh), lambda b, h: (b, h))),
        scratch_shapes=[pltpu.VMEM((tot, CHUNK), F32),
                        pltpu.VMEM((2, nct, dh, 2 * dh), F32),
                        pltpu.VMEM((2, nct, 8, LANES), F32),
                        pltpu.VMEM((2, nct, dh, 2 * dh), BF16),
                        pltpu.VMEM((2, nct, 8, LANES), F32)],
        compiler_params=_cparams(("arbitrary", "arbitrary")),
        name="mlstm",
    )(px, px, px, px, px, pc, pc, pc, pc, pc, *gates_x, *gates_c, g_head)


def _glu(a_ref, g_ref, r0, rows):
    a = a_ref[r0:r0 + rows, :].astype(F32)
    return a * _sigmoid(g_ref[r0:r0 + rows, :].astype(F32))


def _segconv_kernel(a_ref, g_ref, w_ref, b_ref, o_ref, pad_ref, *, seg, sub):
    rows, cw = a_ref.shape
    stride = seg + 2 * SEG_PAD
    total = rows // seg * stride
    zeros = jnp.zeros((SEG_PAD, cw), F32)
    for s in range(rows // seg):
        pad_ref[0, s * stride:s * stride + SEG_PAD, :] = zeros
        pad_ref[0, s * stride + SEG_PAD:s * stride + SEG_PAD + seg, :] = _glu(a_ref, g_ref, s * seg, seg)
        pad_ref[0, s * stride + SEG_PAD + seg:(s + 1) * stride, :] = zeros
    pad_ref[0, total:total + SUBLANES, :] = jnp.zeros((SUBLANES, cw), F32)
    for r in range(1, SUBLANES):
        pad_ref[r, 0:total, :] = pad_ref[0, r:r + total, :]
    for s in range(rows // seg):
        for t0 in range(0, seg, sub):
            start = s * stride + SEG_PAD + t0 - CONV_PAD
            acc = jnp.zeros((sub, cw), F32) + b_ref[...]
            for k in range(CONV_K):
                r = (start + k) % SUBLANES
                acc = acc + pad_ref[r, start + k - r:start + k - r + sub, :] * w_ref[k:k + 1, :]
            o_ref[s * seg + t0:s * seg + t0 + sub, :] = acc


def _segconv(p, w_dw, b_dw, *, seg, rows, cw, ncol, a_col0, g_col0, w_col0):
    m = p.shape[0]
    stride = seg + 2 * SEG_PAD
    return pl.pallas_call(
        functools.partial(_segconv_kernel, seg=seg, sub=min(seg, 64)),
        out_shape=jax.ShapeDtypeStruct((m, ncol * cw), F32),
        grid=(m // rows, ncol),
        in_specs=[pl.BlockSpec((rows, cw), lambda i, j: (i, a_col0 + j)),
                  pl.BlockSpec((rows, cw), lambda i, j: (i, g_col0 + j)),
                  pl.BlockSpec((CONV_K, cw), lambda i, j: (0, w_col0 + j)),
                  pl.BlockSpec((1, cw), lambda i, j: (0, w_col0 + j))],
        out_specs=pl.BlockSpec((rows, cw), lambda i, j: (i, j)),
        scratch_shapes=[pltpu.VMEM((SUBLANES, rows // seg * stride + SUBLANES, cw), F32)],
        compiler_params=_cparams(("arbitrary", "arbitrary")),
        name="segconv",
    )(p, p, w_dw, b_dw)


def _rowconv_kernel(a_ref, g_ref, w_ref, b_ref, o_ref, pad_ref, *, nrows):
    cw = a_ref.shape[1]
    edge = CONV_PAD * GRID_W
    pad_ref[0:edge, :] = jnp.zeros((edge, cw), F32)
    pad_ref[edge + nrows * GRID_W:2 * edge + nrows * GRID_W, :] = jnp.zeros((edge, cw), F32)
    for r in range(nrows):
        pad_ref[edge + r * GRID_W:edge + (r + 1) * GRID_W, :] = _glu(a_ref, g_ref, r * GRID_W, GRID_W)

    def body(r, _):
        acc = jnp.zeros((GRID_W, cw), F32) + b_ref[...]
        for k in range(CONV_K):
            src = pl.multiple_of((r + k) * GRID_W, GRID_W)
            acc = acc + pad_ref[pl.ds(src, GRID_W), :] * w_ref[k:k + 1, :]
        o_ref[pl.ds(pl.multiple_of(r * GRID_W, GRID_W), GRID_W), :] = acc
        return 0
    lax.fori_loop(0, nrows, body, 0)


def _rowconv(p, w_dw, b_dw, *, seq, cw, ncol, a_col0, g_col0, w_col0):
    m = p.shape[0]
    nrows = seq // GRID_W
    return pl.pallas_call(
        functools.partial(_rowconv_kernel, nrows=nrows),
        out_shape=jax.ShapeDtypeStruct((m, ncol * cw), F32),
        grid=(m // seq, ncol),
        in_specs=[pl.BlockSpec((seq, cw), lambda i, j: (i, a_col0 + j)),
                  pl.BlockSpec((seq, cw), lambda i, j: (i, g_col0 + j)),
                  pl.BlockSpec((CONV_K, cw), lambda i, j: (0, w_col0 + j)),
                  pl.BlockSpec((1, cw), lambda i, j: (0, w_col0 + j))],
        out_specs=pl.BlockSpec((seq, cw), lambda i, j: (i, j)),
        scratch_shapes=[pltpu.VMEM((seq + 2 * CONV_PAD * GRID_W, cw), F32)],
        compiler_params=_cparams(("arbitrary", "arbitrary")),
        name="rowconv",
    )(p, p, w_dw, b_dw)


def _out_kernel(cva_ref, cvb_ref, z_ref, ym_ref, x_ref, gt_ref, lng_ref, lnb_ref, gpost_ref,
                wpw_ref, wout_ref, o_ref):
    w_conv = wpw_ref.shape[0]
    u = jnp.concatenate([cva_ref[...], cvb_ref[...]], axis=1)
    mu = jnp.mean(u, axis=-1, keepdims=True)
    uc = u - mu
    r = uc * lax.rsqrt(jnp.mean(uc * uc, axis=-1, keepdims=True) + EPS) * lng_ref[...] + lnb_ref[...]
    t = jnp.dot((r * _sigmoid(r)).astype(BF16), wpw_ref[...], preferred_element_type=F32)
    z = z_ref[...].astype(F32)
    yc = (t * (z * _sigmoid(z))).astype(BF16)
    out = (jnp.dot(yc, wout_ref[0:w_conv, :], preferred_element_type=F32)
           + jnp.dot(ym_ref[...], wout_ref[w_conv:, :], preferred_element_type=F32))
    ms = jnp.mean(out * out, axis=-1, keepdims=True)
    o_ref[...] = x_ref[...] + gt_ref[0] * (out * lax.rsqrt(ms + EPS) * gpost_ref[...])


def _out_proj(cva, cvb, cvb_col, p, ym, x2d, ada3, row_of_tile, ln_g, ln_b, g_post, w_pw2, w_out,
              *, tm, w_conv, layer):
    m, d = x2d.shape
    half = w_conv // 2
    w_mix = w_out.shape[1]
    once = pl.Buffered(1)
    return pl.pallas_call(
        _out_kernel,
        out_shape=jax.ShapeDtypeStruct((m, d), F32),
        grid=(m // tm,),
        in_specs=[pl.BlockSpec((tm, half), lambda i: (i, 0)),
                  pl.BlockSpec((tm, half), lambda i: (i, cvb_col)),
                  pl.BlockSpec((tm, w_conv), lambda i: (i, 2)),
                  pl.BlockSpec((tm, w_mix - w_conv), lambda i: (i, 0)),
                  pl.BlockSpec((tm, d), lambda i: (i, 0)),
                  pl.BlockSpec((1, 1, d), lambda i: (row_of_tile(i) * 3 + 2, 0, 0)),
                  pl.BlockSpec((1, w_conv), lambda i: (0, 0)),
                  pl.BlockSpec((1, w_conv), lambda i: (0, 0)),
                  pl.BlockSpec((1, d), lambda i: (0, 0)),
                  pl.BlockSpec((None, w_conv, w_conv), lambda i: (layer, 0, 0), pipeline_mode=once),
                  pl.BlockSpec((None, w_mix, d), lambda i: (layer, 0, 0), pipeline_mode=once)],
        out_specs=pl.BlockSpec((tm, d), lambda i: (i, 0)),
        compiler_params=_cparams(("arbitrary",)),
        name="out_proj",
    )(cva, cvb, p, ym, x2d, ada3, ln_g, ln_b, g_post, w_pw2, w_out)


def kernel(x, c, ctx, c_ctx, w_ada, b_ada, g_pre, g_post, w_in, b_gate, w_dw, b_dw, ln_g, ln_b, w_pw2,
           g_head, w_out):
    bsz, seq, d = x.shape
    ctx_len = ctx.shape[1]
    depth = w_ada.shape[0]
    w_conv = w_dw.shape[-1]
    w_ml = g_head.shape[-1]
    nh = b_gate.shape[-1] // 4
    n_main = 3 * w_conv + 5 * w_ml
    half = w_conv // 2
    assert w_ml == nh * LANES and 4 * nh <= LANES and w_conv == w_ml
    assert seq % CHUNK == 0 and ctx_len % CHUNK == 0 and seq % GRID_W == 0 and half % LANES == 0

    ctx_row = bsz
    cond = jnp.concatenate([c, c_ctx[None, :], jnp.zeros((8 - bsz - 1, d), F32)], axis=0)
    ada = _ada_all_layers(cond, w_ada, b_ada)
    ada3 = ada.reshape(depth * 8 * 3, 1, d)

    w_in_t = jnp.swapaxes(w_in, 1, 2)
    bias_g = jnp.pad(b_gate, ((0, 0), (0, LANES - 4 * nh)))

    tm_x = min(1024, seq)
    tm_c = min(1024, bsz * ctx_len)
    tn = min(1024, w_conv)
    tm_o = min(512, seq)
    tm_oc = min(256, ctx_len)
    cw = min(256, half)
    w_pw2_b = w_pw2.astype(BF16)
    w_out_b = w_out.astype(BF16)

    xs = x.reshape(bsz * seq, d)
    cs = ctx.reshape(bsz * ctx_len, d)
    for l in range(depth):
        last = l == depth - 1
        row_x = lambda i, l=l, t=seq // tm_x: l * 8 + i // t
        row_c = lambda i, l=l: l * 8 + ctx_row + 0 * i
        pc, gc, w_main_t = _in_proj(cs, ada3, row_c, g_pre[l][None], w_in_t, w_in_t, tm=tm_c,
                                    tn=tn // 2, w_conv=w_conv, w_ml=w_ml, nh=nh, layer=l, emit_w=True)
        px, gx = _in_proj(xs, ada3, row_x, g_pre[l][None], w_main_t, w_in_t,
                          tm=tm_x, tn=tn, w_conv=w_conv, w_ml=w_ml, nh=nh, layer=l)
        gates_x = _gate_prep(gx, bias_g[l][None], nh, min(1024, seq))
        gates_c = _gate_prep(gc, bias_g[l][None], nh, min(1024, ctx_len))
        ymx, ymc = _mlstm(px, pc, gates_x, gates_c, g_head[l][None], bsz=bsz, seq=seq,
                          ctx_len=ctx_len, nh=nh, w_conv=w_conv, w_ml=w_ml)
        ncol = half // cw
        g0 = w_conv // cw
        cv_w = _segconv(px, w_dw[l], b_dw[l][None], seg=GRID_W, rows=min(512, seq), cw=cw, ncol=ncol,
                        a_col0=0, g_col0=g0, w_col0=0)
        cv_h = _rowconv(px, w_dw[l], b_dw[l][None], seq=seq, cw=cw, ncol=ncol,
                        a_col0=ncol, g_col0=g0 + ncol, w_col0=ncol)
        row_xo = lambda i, l=l, t=seq // tm_o: l * 8 + i // t
        xs_new = _out_proj(cv_w, cv_h, 0, px, ymx, xs, ada3, row_xo, ln_g[l][None], ln_b[l][None],
                           g_post[l][None], w_pw2_b, w_out_b, tm=tm_o, w_conv=w_conv, layer=l)
        if not last:
            cv_c = _segconv(pc, w_dw[l], b_dw[l][None], seg=ctx_len, rows=ctx_len, cw=cw,
                            ncol=2 * ncol, a_col0=0, g_col0=g0, w_col0=0)
            cs = _out_proj(cv_c, cv_c, 1, pc, ymc, cs, ada3, row_c, ln_g[l][None], ln_b[l][None],
                           g_post[l][None], w_pw2_b, w_out_b, tm=tm_oc, w_conv=w_conv, layer=l)
        xs = xs_new
    return xs.reshape(bsz, seq, d)
```

```python
import functools

import jax
import jax.numpy as jnp
from jax import lax
from jax.experimental import pallas as pl
from jax.experimental.pallas import tpu as pltpu

F32 = jnp.float32
BF16 = jnp.bfloat16

GRID_W = 64
CONV_K = 31
CONV_PAD = CONV_K // 2
EPS = 1e-6
NEG = -1e30
LOG2E = 1.4426950408889634
LANES = 128
SUBLANES = 8
CHUNK = 128
SEG_PAD = 16
VMEM_LIMIT = 56 * 1024 * 1024


def _cparams(sem):
    return pltpu.CompilerParams(dimension_semantics=sem, vmem_limit_bytes=VMEM_LIMIT)


def _sigmoid(x):
    return 1.0 / (1.0 + jnp.exp(-x))


def _ada_kernel(c_ref, w_ref, b_ref, o_ref):
    c = c_ref[...]
    s = (c * _sigmoid(c)).astype(BF16)
    o_ref[0] = jnp.dot(s, w_ref[0].astype(BF16), preferred_element_type=F32) + b_ref[0]


def _ada_all_layers(cond, w_ada, b_ada):
    depth, d, n = w_ada.shape
    tn = 1024 if n % 1024 == 0 else n // 3
    return pl.pallas_call(
        _ada_kernel,
        out_shape=jax.ShapeDtypeStruct((depth, 8, n), F32),
        grid=(depth, n // tn),
        in_specs=[pl.BlockSpec((8, d), lambda l, j: (0, 0)),
                  pl.BlockSpec((1, d, tn), lambda l, j: (l, 0, j)),
                  pl.BlockSpec((1, 1, tn), lambda l, j: (l, 0, j))],
        out_specs=pl.BlockSpec((1, 8, tn), lambda l, j: (l, 0, j)),
        compiler_params=_cparams(("arbitrary", "arbitrary")),
        name="ada",
    )(cond, w_ada, b_ada.reshape(depth, 1, n))


_NT = (((1,), (1,)), ((), ()))


def _in_kernel(x_ref, sh_ref, sc_ref, gpre_ref, w_ref, wg_ref, p_ref, g_ref, *rest,
               tn, q_lo, q_hi, q_scale, sub):
    hx_ref = rest[-1]
    j = pl.program_id(1)

    @pl.when(j == 0)
    def _():
        tm, d = x_ref.shape
        for r in range(0, tm, sub):
            x = x_ref[r:r + sub, :]
            ms = jnp.mean(x * x, axis=-1, keepdims=True)
            h = (x * lax.rsqrt(ms + EPS) * gpre_ref[...]) * (1.0 + sc_ref[0]) + sh_ref[0]
            hx_ref[r:r + sub, :] = h.astype(BF16)
        wg = wg_ref[...].astype(BF16)
        wg = jnp.concatenate([wg, jnp.zeros((LANES - wg.shape[0], d), BF16)], axis=0)
        g_ref[...] = lax.dot_general(hx_ref[...], wg, _NT, preferred_element_type=F32)

    w = w_ref[...]
    if len(rest) == 2:
        w = w.astype(BF16)
        rest[0][...] = w
    acc = lax.dot_general(hx_ref[...], w, _NT, preferred_element_type=F32)
    col = j * tn + lax.broadcasted_iota(jnp.int32, (1, tn), 1)
    scale = jnp.where((col >= q_lo) & (col < q_hi), q_scale, 1.0).astype(F32)
    p_ref[...] = (acc * scale).astype(BF16)


def _in_proj(x2d, ada3, row_of_tile, g_pre, w_t, w_in_t, *, tm, tn, w_conv, w_ml, nh, layer,
             emit_w=False):
    m, d = x2d.shape
    n = 3 * w_conv + 5 * w_ml
    q_lo = 3 * w_conv
    kern = functools.partial(_in_kernel, tn=tn, q_lo=q_lo, q_hi=q_lo + w_ml,
                             q_scale=float(LANES) ** -0.5, sub=min(tm, 256))
    out_shape = [jax.ShapeDtypeStruct((m, n), BF16), jax.ShapeDtypeStruct((m, LANES), F32)]
    out_specs = [pl.BlockSpec((tm, tn), lambda i, j: (i, j)),
                 pl.BlockSpec((tm, LANES), lambda i, j: (i, 0))]
    if emit_w:
        assert m == tm
        w_spec = pl.BlockSpec((None, tn, d), lambda i, j: (layer, j, 0))
        out_shape.append(jax.ShapeDtypeStruct((n, d), BF16))
        out_specs.append(pl.BlockSpec((tn, d), lambda i, j: (j, 0)))
    else:
        w_spec = pl.BlockSpec((tn, d), lambda i, j: (j, 0))
    return pl.pallas_call(
        kern,
        out_shape=tuple(out_shape),
        grid=(m // tm, n // tn),
        in_specs=[pl.BlockSpec((tm, d), lambda i, j: (i, 0)),
                  pl.BlockSpec((1, 1, d), lambda i, j: (row_of_tile(i) * 3 + 0, 0, 0)),
                  pl.BlockSpec((1, 1, d), lambda i, j: (row_of_tile(i) * 3 + 1, 0, 0)),
                  pl.BlockSpec((1, d), lambda i, j: (0, 0)),
                  w_spec,
                  pl.BlockSpec((None, 4 * nh, d), lambda i, j: (layer, n // (4 * nh), 0))],
        out_specs=tuple(out_specs),
        scratch_shapes=[pltpu.VMEM((tm, d), BF16)],
        compiler_params=_cparams(("arbitrary", "arbitrary")),
        name="in_proj",
    )(x2d, ada3, ada3, g_pre, w_t, w_in_t)


def _gate_kernel(g_ref, bias_ref, bt_ref, rt_ref, ct_ref, *, nh):
    row = lax.broadcasted_iota(jnp.int32, (CHUNK, CHUNK), 0)
    col = lax.broadcasted_iota(jnp.int32, (CHUNK, CHUNK), 1)
    tril = jnp.where(row >= col, 1.0, 0.0).astype(BF16)
    triu = jnp.where(row <= col, 1.0, 0.0).astype(BF16)
    is_ff = (col >= nh) & (col < 2 * nh)
    is_fb = (col >= 3 * nh) & (col < 4 * nh)
    grow = lax.broadcasted_iota(jnp.int32, (4 * nh, CHUNK), 0)
    tok = lax.broadcasted_iota(jnp.int32, (4 * nh, CHUNK), 1)

    def csum(tri, parts):
        out = jnp.dot(tri, parts[0], preferred_element_type=F32)
        for p in parts[1:]:
            out = out + jnp.dot(tri, p, preferred_element_type=F32)
        return out

    for c in range(g_ref.shape[0] // CHUNK):
        g = g_ref[c * CHUNK:(c + 1) * CHUNK, :] + bias_ref[...]
        lf = jnp.minimum(g, 0.0) - jnp.log(1.0 + jnp.exp(-jnp.abs(g)))
        hi = lf.astype(BF16)
        r1 = lf - hi.astype(F32)
        mid = r1.astype(BF16)
        lo = (r1 - mid.astype(F32)).astype(BF16)
        parts = (hi, mid, lo)
        bmat = jnp.where(is_ff, csum(tril, parts), jnp.where(is_fb, csum(triu, parts), 0.0))
        li = pltpu.roll(g, nh, axis=1)
        rmat = li - bmat
        bt_ref[c] = bmat.T[0:4 * nh, :] * LOG2E
        rt = rmat.T[0:4 * nh, :] * LOG2E
        rt_ref[c] = rt
        cf = rt
        cb = rt
        sh = 1
        while sh < CHUNK:
            cf = jnp.maximum(cf, jnp.where(tok >= sh, pltpu.roll(cf, sh, axis=1), NEG))
            cb = jnp.maximum(cb, jnp.where(tok < CHUNK - sh, pltpu.roll(cb, CHUNK - sh, axis=1), NEG))
            sh *= 2
        ct_ref[c] = jnp.where(grow < 2 * nh, cf, cb).astype(BF16).astype(F32)


def _gate_prep(g2d, bias, nh, rows):
    m = g2d.shape[0]
    nc = rows // CHUNK
    out = jax.ShapeDtypeStruct((m // CHUNK, 4 * nh, CHUNK), F32)
    ospec = pl.BlockSpec((nc, 4 * nh, CHUNK), lambda i: (i, 0, 0))
    return pl.pallas_call(
        functools.partial(_gate_kernel, nh=nh),
        out_shape=(out, out, out),
        grid=(m // rows,),
        in_specs=[pl.BlockSpec((rows, LANES), lambda i: (i, 0)),
                  pl.BlockSpec((1, LANES), lambda i: (0, 0))],
        out_specs=(ospec, ospec, ospec),
        compiler_params=_cparams(("arbitrary",)),
        name="gate_prep",
    )(g2d, bias)


def _mlstm_kernel(qx_ref, kx_ref, vx_ref, ox_ref, zx_ref, qc_ref, kc_ref, vc_ref, oc_ref, zc_ref,
                  btx_ref, rtx_ref, ctx_ref, btc_ref, rtc_ref, ctc_ref, gh_ref, yx_ref, yc_ref,
                  s_s, kv_s, sc_s, cp_s, mp_s, *, nh, group):
    h_idx = pl.program_id(1)
    row = lax.broadcasted_iota(jnp.int32, (CHUNK, CHUNK), 0)
    col = lax.broadcasted_iota(jnp.int32, (CHUNK, CHUNK), 1)
    masks = (row >= col, row <= col)
    ones_blk = jnp.ones((CHUNK, LANES), BF16)
    eye = row == col
    row2 = lax.broadcasted_iota(jnp.int32, (2 * CHUNK, 2 * LANES), 0)
    col2 = lax.broadcasted_iota(jnp.int32, (2 * CHUNK, 2 * LANES), 1)
    ones2 = jnp.where((row2 < CHUNK) == (col2 < LANES), 1.0, 0.0).astype(BF16)
    ncx = qx_ref.shape[0] // CHUNK
    ncc = qc_ref.shape[0] // CHUNK
    nct = ncc + ncx
    g_rows = (nh + h_idx, 3 * nh + h_idx)
    last = (CHUNK - 1, 0)

    trips = ncx // group
    ctx_per_trip = ncc // trips

    def both_segments(chunk_fn, latent_refs, context_refs):
        def body(i, _):
            for u in range(group):
                chunk_fn(i * group + u, ncc, *latent_refs)
            for u in range(ctx_per_trip):
                chunk_fn(i * ctx_per_trip + u, 0, *context_refs)
            return 0
        lax.fori_loop(0, trips, body, 0)

    def chunk_a(c, c0, q_ref, k_ref, v_ref, bt_ref, rt_ref, ct_ref):
        r0 = pl.multiple_of(c * CHUNK, CHUNK)
        rs = pl.multiple_of((c0 + c) * CHUNK, CHUNK)
        kt = k_ref[pl.ds(r0, CHUNK), :].astype(F32).T
        s_s[pl.ds(rs, CHUNK), :] = jnp.dot(q_ref[pl.ds(r0, CHUNK), :], kt.astype(BF16),
                                           preferred_element_type=F32)
        vaug = jnp.concatenate([v_ref[pl.ds(r0, CHUNK), :], ones_blk], axis=1)
        for d in range(2):
            b_last = bt_ref[c, pl.ds(g_rows[d], 1), last[d]:last[d] + 1]
            mg = b_last + ct_ref[c, pl.ds(g_rows[d], 1), last[d]:last[d] + 1]
            gl = b_last + rt_ref[c, pl.ds(g_rows[d], 1), :]
            kws = (kt * jnp.exp2(gl - mg)).astype(BF16)
            kv_s[d, c0 + c] = jnp.dot(kws, vaug, preferred_element_type=F32)
            sc_s[d, c0 + c, 0:1, :] = jnp.broadcast_to(b_last, (1, LANES))
            sc_s[d, c0 + c, 1:2, :] = jnp.broadcast_to(mg, (1, LANES))

    both_segments(chunk_a, (qx_ref, kx_ref, vx_ref, btx_ref, rtx_ref, ctx_ref),
                  (qc_ref, kc_ref, vc_ref, btc_ref, rtc_ref, ctc_ref))

    def scan(d):
        def body(j, carry):
            m_prev, c_prev = carry
            g = j if d == 0 else jnp.where(j < ncc, ncc - 1 - j, nct - 1 - j + ncc)
            cp_s[d, g] = c_prev.astype(BF16)
            mp_s[d, g] = jnp.broadcast_to(m_prev, (8, LANES))
            b_last = sc_s[d, g, 0:1, :]
            mg = sc_s[d, g, 1:2, :]
            m_new = jnp.maximum(b_last + m_prev, mg)
            a = jnp.exp2(b_last + m_prev - m_new)
            w = jnp.exp2(mg - m_new)
            c_new = (jnp.concatenate([a, a], axis=1) * c_prev
                     + jnp.concatenate([w, w], axis=1) * kv_s[d, g])
            return (m_new, c_new)
        lax.fori_loop(0, nct, body, (jnp.full((1, LANES), NEG, F32),
                                     jnp.zeros((LANES, 2 * LANES), F32)), unroll=2)

    scan(0)
    scan(1)

    def chunk_b(c, c0, q_ref, v_ref, o_ref, z_ref, bt_ref, rt_ref, ct_ref, y_ref):
        r0 = pl.multiple_of(c * CHUNK, CHUNK)
        rs = pl.multiple_of((c0 + c) * CHUNK, CHUNK)
        q = q_ref[pl.ds(r0, CHUNK), :].astype(F32)
        s = s_s[pl.ds(rs, CHUNK), :]
        vaug = jnp.concatenate([v_ref[pl.ds(r0, CHUNK), :], ones_blk], axis=1)
        h = None
        for d in range(2):
            c_row = ct_ref[c, pl.ds(g_rows[d], 1), :]
            r_row = rt_ref[c, pl.ds(g_rows[d], 1), :]
            m_prev = mp_s[d, c0 + c, 0:1, :]
            floor_row = jnp.exp2(-(bt_ref[c, pl.ds(g_rows[d], 1), :] + jnp.maximum(c_row, m_prev)))
            diag = jnp.concatenate([jnp.where(eye, c_row, 0.0), jnp.where(eye, floor_row, 0.0)],
                                   axis=1).astype(BF16)
            cols = jnp.dot(diag, ones2, preferred_element_type=F32)
            cm_bc = cols[:, 0:LANES]
            floor_bc = cols[:, LANES:]
            mm = jnp.maximum(cm_bc, m_prev)
            qw = (q * jnp.exp2(m_prev - mm)).astype(BF16)
            pm = (jnp.exp2(jnp.where(masks[d], r_row - mm, NEG)) * s).astype(BF16)
            nd = jnp.dot(jnp.concatenate([qw, pm], axis=1),
                         jnp.concatenate([cp_s[d, c0 + c], vaug], axis=0),
                         preferred_element_type=F32)
            hd = nd[:, 0:LANES] / jnp.maximum(jnp.abs(nd[:, LANES:]), floor_bc)
            h = hd if h is None else h + hd
        mu = jnp.mean(h, axis=1, keepdims=True)
        hc = h - mu
        hn = hc * lax.rsqrt(jnp.mean(hc * hc, axis=1, keepdims=True) + EPS) * gh_ref[...]
        z = z_ref[pl.ds(r0, CHUNK), :].astype(F32)
        o = o_ref[pl.ds(r0, CHUNK), :].astype(F32)
        y = hn * (z / ((1.0 + jnp.exp(-o)) * (1.0 + jnp.exp(-z))))
        y_ref[pl.ds(r0, CHUNK), :] = y.astype(BF16)

    both_segments(chunk_b, (qx_ref, vx_ref, ox_ref, zx_ref, btx_ref, rtx_ref, ctx_ref, yx_ref),
                  (qc_ref, vc_ref, oc_ref, zc_ref, btc_ref, rtc_ref, ctc_ref, yc_ref))


def _mlstm(px, pc, gates_x, gates_c, g_head, *, bsz, seq, ctx_len, nh, w_conv, w_ml):
    dh = LANES
    cb = 3 * w_conv // dh
    hb = w_ml // dh

    def colspec(rows, k):
        return pl.BlockSpec((rows, dh), lambda b, h: (b, cb + k * hb + h))

    def gspec(rows):
        return pl.BlockSpec((rows // CHUNK, 4 * nh, CHUNK), lambda b, h: (b, 0, 0))

    tot = seq + ctx_len
    nct = tot // CHUNK
    group = min(8, seq // CHUNK)
    assert (seq // CHUNK) % group == 0 and (ctx_len // CHUNK) % (seq // CHUNK // group) == 0
    return pl.pallas_call(
        functools.partial(_mlstm_kernel, nh=nh, group=group),
        out_shape=(jax.ShapeDtypeStruct((bsz * seq, w_ml), BF16),
                   jax.ShapeDtypeStruct((bsz * ctx_len, w_ml), BF16)),
        grid=(bsz, nh),
        in_specs=[colspec(seq, 0), colspec(seq, 1), colspec(seq, 2), colspec(seq, 3), colspec(seq, 4),
                  colspec(ctx_len, 0), colspec(ctx_len, 1), colspec(ctx_len, 2), colspec(ctx_len, 3),
                  colspec(ctx_len, 4),
                  gspec(seq), gspec(seq), gspec(seq), gspec(ctx_len), gspec(ctx_len), gspec(ctx_len),
                  pl.BlockSpec((1, dh), lambda b, h: (0, h))],
        out_specs=(pl.BlockSpec((seq, dh), lambda b, h: (b, h)),
                   pl.BlockSpec((ctx_len, dh), lambda b, h: (b, h))),
        scratch_shapes=[pltpu.VMEM((tot, CHUNK), F32),
                        pltpu.VMEM((2, nct, dh, 2 * dh), F32),
                        pltpu.VMEM((2, nct, 8, LANES), F32),
                        pltpu.VMEM((2, nct, dh, 2 * dh), BF16),
                        pltpu.VMEM((2, nct, 8, LANES), F32)],
        compiler_params=_cparams(("arbitrary", "arbitrary")),
        name="mlstm",
    )(px, px, px, px, px, pc, pc, pc, pc, pc, *gates_x, *gates_c, g_head)


def _glu(a_ref, g_ref, r0, rows):
    a = a_ref[r0:r0 + rows, :].astype(F32)
    return a * _sigmoid(g_ref[r0:r0 + rows, :].astype(F32))


def _segconv_kernel(a_ref, g_ref, w_ref, b_ref, o_ref, pad_ref, *, seg, sub):
    rows, cw = a_ref.shape
    stride = seg + 2 * SEG_PAD
    total = rows // seg * stride
    zeros = jnp.zeros((SEG_PAD, cw), F32)
    for s in range(rows // seg):
        pad_ref[0, s * stride:s * stride + SEG_PAD, :] = zeros
        pad_ref[0, s * stride + SEG_PAD:s * stride + SEG_PAD + seg, :] = _glu(a_ref, g_ref, s * seg, seg)
        pad_ref[0, s * stride + SEG_PAD + seg:(s + 1) * stride, :] = zeros
    pad_ref[0, total:total + SUBLANES, :] = jnp.zeros((SUBLANES, cw), F32)
    for r in range(1, SUBLANES):
        pad_ref[r, 0:total, :] = pad_ref[0, r:r + total, :]
    for s in range(rows // seg):
        for t0 in range(0, seg, sub):
            start = s * stride + SEG_PAD + t0 - CONV_PAD
            acc = jnp.zeros((sub, cw), F32) + b_ref[...]
            for k in range(CONV_K):
                r = (start + k) % SUBLANES
                acc = acc + pad_ref[r, start + k - r:start + k - r + sub, :] * w_ref[k:k + 1, :]
            o_ref[s * seg + t0:s * seg + t0 + sub, :] = acc


def _segconv(p, w_dw, b_dw, *, seg, rows, cw, ncol, a_col0, g_col0, w_col0):
    m = p.shape[0]
    stride = seg + 2 * SEG_PAD
    return pl.pallas_call(
        functools.partial(_segconv_kernel, seg=seg, sub=min(seg, 64)),
        out_shape=jax.ShapeDtypeStruct((m, ncol * cw), F32),
        grid=(m // rows, ncol),
        in_specs=[pl.BlockSpec((rows, cw), lambda i, j: (i, a_col0 + j)),
                  pl.BlockSpec((rows, cw), lambda i, j: (i, g_col0 + j)),
                  pl.BlockSpec((CONV_K, cw), lambda i, j: (0, w_col0 + j)),
                  pl.BlockSpec((1, cw), lambda i, j: (0, w_col0 + j))],
        out_specs=pl.BlockSpec((rows, cw), lambda i, j: (i, j)),
        scratch_shapes=[pltpu.VMEM((SUBLANES, rows // seg * stride + SUBLANES, cw), F32)],
        compiler_params=_cparams(("arbitrary", "arbitrary")),
        name="segconv",
    )(p, p, w_dw, b_dw)


def _rowconv_kernel(a_ref, g_ref, w_ref, b_ref, o_ref, pad_ref, *, nrows):
    cw = a_ref.shape[1]
    edge = CONV_PAD * GRID_W
    pad_ref[0:edge, :] = jnp.zeros((edge, cw), F32)
    pad_ref[edge + nrows * GRID_W:2 * edge + nrows * GRID_W, :] = jnp.zeros((edge, cw), F32)
    for r in range(nrows):
        pad_ref[edge + r * GRID_W:edge + (r + 1) * GRID_W, :] = _glu(a_ref, g_ref, r * GRID_W, GRID_W)

    def body(r, _):
        acc = jnp.zeros((GRID_W, cw), F32) + b_ref[...]
        for k in range(CONV_K):
            src = pl.multiple_of((r + k) * GRID_W, GRID_W)
            acc = acc + pad_ref[pl.ds(src, GRID_W), :] * w_ref[k:k + 1, :]
        o_ref[pl.ds(pl.multiple_of(r * GRID_W, GRID_W), GRID_W), :] = acc
        return 0
    lax.fori_loop(0, nrows, body, 0)


def _rowconv(p, w_dw, b_dw, *, seq, cw, ncol, a_col0, g_col0, w_col0):
    m = p.shape[0]
    nrows = seq // GRID_W
    return pl.pallas_call(
        functools.partial(_rowconv_kernel, nrows=nrows),
        out_shape=jax.ShapeDtypeStruct((m, ncol * cw), F32),
        grid=(m // seq, ncol),
        in_specs=[pl.BlockSpec((seq, cw), lambda i, j: (i, a_col0 + j)),
                  pl.BlockSpec((seq, cw), lambda i, j: (i, g_col0 + j)),
                  pl.BlockSpec((CONV_K, cw), lambda i, j: (0, w_col0 + j)),
                  pl.BlockSpec((1, cw), lambda i, j: (0, w_col0 + j))],
        out_specs=pl.BlockSpec((seq, cw), lambda i, j: (i, j)),
        scratch_shapes=[pltpu.VMEM((seq + 2 * CONV_PAD * GRID_W, cw), F32)],
        compiler_params=_cparams(("arbitrary", "arbitrary")),
        name="rowconv",
    )(p, p, w_dw, b_dw)


def _out_kernel(cva_ref, cvb_ref, z_ref, ym_ref, x_ref, gt_ref, lng_ref, lnb_ref, gpost_ref,
                wpw_ref, wout_ref, o_ref):
    w_conv = wpw_ref.shape[0]
    u = jnp.concatenate([cva_ref[...], cvb_ref[...]], axis=1)
    mu = jnp.mean(u, axis=-1, keepdims=True)
    uc = u - mu
    r = uc * lax.rsqrt(jnp.mean(uc * uc, axis=-1, keepdims=True) + EPS) * lng_ref[...] + lnb_ref[...]
    t = jnp.dot((r * _sigmoid(r)).astype(BF16), wpw_ref[...], preferred_element_type=F32)
    z = z_ref[...].astype(F32)
    yc = (t * (z * _sigmoid(z))).astype(BF16)
    out = (jnp.dot(yc, wout_ref[0:w_conv, :], preferred_element_type=F32)
           + jnp.dot(ym_ref[...], wout_ref[w_conv:, :], preferred_element_type=F32))
    ms = jnp.mean(out * out, axis=-1, keepdims=True)
    o_ref[...] = x_ref[...] + gt_ref[0] * (out * lax.rsqrt(ms + EPS) * gpost_ref[...])


def _out_proj(cva, cvb, cvb_col, p, ym, x2d, ada3, row_of_tile, ln_g, ln_b, g_post, w_pw2, w_out,
              *, tm, w_conv, layer):
    m, d = x2d.shape
    half = w_conv // 2
    w_mix = w_out.shape[1]
    once = pl.Buffered(1)
    return pl.pallas_call(
        _out_kernel,
        out_shape=jax.ShapeDtypeStruct((m, d), F32),
        grid=(m // tm,),
        in_specs=[pl.BlockSpec((tm, half), lambda i: (i, 0)),
                  pl.BlockSpec((tm, half), lambda i: (i, cvb_col)),
                  pl.BlockSpec((tm, w_conv), lambda i: (i, 2)),
                  pl.BlockSpec((tm, w_mix - w_conv), lambda i: (i, 0)),
                  pl.BlockSpec((tm, d), lambda i: (i, 0)),
                  pl.BlockSpec((1, 1, d), lambda i: (row_of_tile(i) * 3 + 2, 0, 0)),
                  pl.BlockSpec((1, w_conv), lambda i: (0, 0)),
                  pl.BlockSpec((1, w_conv), lambda i: (0, 0)),
                  pl.BlockSpec((1, d), lambda i: (0, 0)),
                  pl.BlockSpec((None, w_conv, w_conv), lambda i: (layer, 0, 0), pipeline_mode=once),
                  pl.BlockSpec((None, w_mix, d), lambda i: (layer, 0, 0), pipeline_mode=once)],
        out_specs=pl.BlockSpec((tm, d), lambda i: (i, 0)),
        compiler_params=_cparams(("arbitrary",)),
        name="out_proj",
    )(cva, cvb, p, ym, x2d, ada3, ln_g, ln_b, g_post, w_pw2, w_out)


def kernel(x, c, ctx, c_ctx, w_ada, b_ada, g_pre, g_post, w_in, b_gate, w_dw, b_dw, ln_g, ln_b, w_pw2,
           g_head, w_out):
    bsz, seq, d = x.shape
    ctx_len = ctx.shape[1]
    depth = w_ada.shape[0]
    w_conv = w_dw.shape[-1]
    w_ml = g_head.shape[-1]
    nh = b_gate.shape[-1] // 4
    n_main = 3 * w_conv + 5 * w_ml
    half = w_conv // 2
    assert w_ml == nh * LANES and 4 * nh <= LANES and w_conv == w_ml
    assert seq % CHUNK == 0 and ctx_len % CHUNK == 0 and seq % GRID_W == 0 and half % LANES == 0

    ctx_row = bsz
    cond = jnp.concatenate([c, c_ctx[None, :], jnp.zeros((8 - bsz - 1, d), F32)], axis=0)
    ada = _ada_all_layers(cond, w_ada, b_ada)
    ada3 = ada.reshape(depth * 8 * 3, 1, d)

    w_in_t = jnp.swapaxes(w_in, 1, 2)
    bias_g = jnp.pad(b_gate, ((0, 0), (0, LANES - 4 * nh)))

    tm_x = min(1024, seq)
    tm_c = min(1024, bsz * ctx_len)
    tn = min(1024, w_conv)
    tm_o = min(512, seq)
    tm_oc = min(256, ctx_len)
    cw = min(256, half)
    w_pw2_b = w_pw2.astype(BF16)
    w_out_b = w_out.astype(BF16)

    xs = x.reshape(bsz * seq, d)
    cs = ctx.reshape(bsz * ctx_len, d)
    for l in range(depth):
        last = l == depth - 1
        row_x = lambda i, l=l, t=seq // tm_x: l * 8 + i // t
        row_c = lambda i, l=l: l * 8 + ctx_row + 0 * i
        pc, gc, w_main_t = _in_proj(cs, ada3, row_c, g_pre[l][None], w_in_t, w_in_t, tm=tm_c,
                                    tn=tn // 2, w_conv=w_conv, w_ml=w_ml, nh=nh, layer=l, emit_w=True)
        px, gx = _in_proj(xs, ada3, row_x, g_pre[l][None], w_main_t, w_in_t,
                          tm=tm_x, tn=2 * tn, w_conv=w_conv, w_ml=w_ml, nh=nh, layer=l)
        gates_x = _gate_prep(gx, bias_g[l][None], nh, min(1024, seq))
        gates_c = _gate_prep(gc, bias_g[l][None], nh, min(1024, ctx_len))
        ymx, ymc = _mlstm(px, pc, gates_x, gates_c, g_head[l][None], bsz=bsz, seq=seq,
                          ctx_len=ctx_len, nh=nh, w_conv=w_conv, w_ml=w_ml)
        ncol = half // cw
        g0 = w_conv // cw
        cv_w = _segconv(px, w_dw[l], b_dw[l][None], seg=GRID_W, rows=min(512, seq), cw=cw, ncol=ncol,
                        a_col0=0, g_col0=g0, w_col0=0)
        cv_h = _rowconv(px, w_dw[l], b_dw[l][None], seq=seq, cw=cw, ncol=ncol,
                        a_col0=ncol, g_col0=g0 + ncol, w_col0=ncol)
        row_xo = lambda i, l=l, t=seq // tm_o: l * 8 + i // t
        xs_new = _out_proj(cv_w, cv_h, 0, px, ymx, xs, ada3, row_xo, ln_g[l][None], ln_b[l][None],
                           g_post[l][None], w_pw2_b, w_out_b, tm=tm_o, w_conv=w_conv, layer=l)
        if not last:
            cv_c = _segconv(pc, w_dw[l], b_dw[l][None], seg=ctx_len, rows=ctx_len, cw=cw,
                            ncol=2 * ncol, a_col0=0, g_col0=g0, w_col0=0)
            cs = _out_proj(cv_c, cv_c, 1, pc, ymc, cs, ada3, row_c, ln_g[l][None], ln_b[l][None],
                           g_post[l][None], w_pw2_b, w_out_b, tm=tm_oc, w_conv=w_conv, layer=l)
        xs = xs_new
    return xs.reshape(bsz, seq, d)
```

```python
import functools

import jax
import jax.numpy as jnp
from jax import lax
from jax.experimental import pallas as pl
from jax.experimental.pallas import tpu as pltpu

F32 = jnp.float32
BF16 = jnp.bfloat16

GRID_W = 64
CONV_K = 31
CONV_PAD = CONV_K // 2
EPS = 1e-6
NEG = -1e30
LOG2E = 1.4426950408889634
LANES = 128
SUBLANES = 8
CHUNK = 128
SEG_PAD = 16
VMEM_LIMIT = 56 * 1024 * 1024


def _cparams(sem):
    return pltpu.CompilerParams(dimension_semantics=sem, vmem_limit_bytes=VMEM_LIMIT)


def _sigmoid(x):
    return 1.0 / (1.0 + jnp.exp(-x))


def _ada_kernel(c_ref, w_ref, b_ref, o_ref):
    c = c_ref[...]
    s = (c * _sigmoid(c)).astype(BF16)
    o_ref[0] = jnp.dot(s, w_ref[0].astype(BF16), preferred_element_type=F32) + b_ref[0]


def _ada_all_layers(cond, w_ada, b_ada):
    depth, d, n = w_ada.shape
    tn = 1024 if n % 1024 == 0 else n // 3
    return pl.pallas_call(
        _ada_kernel,
        out_shape=jax.ShapeDtypeStruct((depth, 8, n), F32),
        grid=(depth, n // tn),
        in_specs=[pl.BlockSpec((8, d), lambda l, j: (0, 0)),
                  pl.BlockSpec((1, d, tn), lambda l, j: (l, 0, j)),
                  pl.BlockSpec((1, 1, tn), lambda l, j: (l, 0, j))],
        out_specs=pl.BlockSpec((1, 8, tn), lambda l, j: (l, 0, j)),
        compiler_params=_cparams(("arbitrary", "arbitrary")),
        name="ada",
    )(cond, w_ada, b_ada.reshape(depth, 1, n))


_NT = (((1,), (1,)), ((), ()))


def _in_kernel(x_ref, sh_ref, sc_ref, gpre_ref, w_ref, wg_ref, *rest,
               tn, q_lo, q_hi, q_scale, sub, emit_w):
    if emit_w:
        wpw32_ref, wout32_ref, p_ref, g_ref, wb_ref, wpwb_ref, woutb_ref, hx_ref = rest
        wpwb_ref[...] = wpw32_ref[...].astype(BF16)
        woutb_ref[...] = wout32_ref[...].astype(BF16)
    else:
        p_ref, g_ref, hx_ref = rest
    j = pl.program_id(1)

    @pl.when(j == 0)
    def _():
        tm, d = x_ref.shape
        gain = gpre_ref[...] * (1.0 + sc_ref[0])
        for r in range(0, tm, sub):
            x = x_ref[r:r + sub, :]
            ms = jnp.mean(x * x, axis=-1, keepdims=True)
            h = (x * lax.rsqrt(ms + EPS)) * gain + sh_ref[0]
            hx_ref[r:r + sub, :] = h.astype(BF16)
        wg = wg_ref[...].astype(BF16)
        wg = jnp.concatenate([wg, jnp.zeros((LANES - wg.shape[0], d), BF16)], axis=0)
        g_ref[...] = lax.dot_general(hx_ref[...], wg, _NT, preferred_element_type=F32)

    w = w_ref[...]
    if emit_w:
        w = w.astype(BF16)
        wb_ref[...] = w
    acc = lax.dot_general(hx_ref[...], w, _NT, preferred_element_type=F32)
    col = j * tn + lax.broadcasted_iota(jnp.int32, (1, tn), 1)
    scale = jnp.where((col >= q_lo) & (col < q_hi), q_scale, 1.0).astype(F32)
    p_ref[...] = (acc * scale).astype(BF16)


def _in_proj(x2d, ada3, row_of_tile, g_pre, w_t, w_in_t, *, tm, tn, w_conv, w_ml, nh, layer,
             cast_also=None):
    m, d = x2d.shape
    n = 3 * w_conv + 5 * w_ml
    q_lo = 3 * w_conv
    steps = n // tn
    emit_w = cast_also is not None
    kern = functools.partial(_in_kernel, tn=tn, q_lo=q_lo, q_hi=q_lo + w_ml,
                             q_scale=float(LANES) ** -0.5, sub=min(tm, 256), emit_w=emit_w)
    in_specs = [pl.BlockSpec((tm, d), lambda i, j: (i, 0)),
                pl.BlockSpec((1, 1, d), lambda i, j: (row_of_tile(i) * 3 + 0, 0, 0)),
                pl.BlockSpec((1, 1, d), lambda i, j: (row_of_tile(i) * 3 + 1, 0, 0)),
                pl.BlockSpec((1, d), lambda i, j: (0, 0)),
                pl.BlockSpec((tn, d), lambda i, j: (j, 0)),
                pl.BlockSpec((None, 4 * nh, d), lambda i, j: (layer, n // (4 * nh), 0))]
    args = [x2d, ada3, ada3, g_pre, w_t, w_in_t]
    out_shape = [jax.ShapeDtypeStruct((m, n), BF16), jax.ShapeDtypeStruct((m, LANES), F32)]
    out_specs = [pl.BlockSpec((tm, tn), lambda i, j: (i, j)),
                 pl.BlockSpec((tm, LANES), lambda i, j: (i, 0))]
    if emit_w:
        assert m == tm
        in_specs[4] = pl.BlockSpec((None, tn, d), lambda i, j: (layer, j, 0))
        args[4] = w_in_t
        out_shape.append(jax.ShapeDtypeStruct((n, d), BF16))
        out_specs.append(pl.BlockSpec((tn, d), lambda i, j: (j, 0)))
        for w32 in cast_also:
            rows, cols = w32.shape[1] // steps, w32.shape[2]
            in_specs.append(pl.BlockSpec((None, rows, cols), lambda i, j: (layer, j, 0)))
            args.append(w32)
            out_shape.append(jax.ShapeDtypeStruct(w32.shape[1:], BF16))
            out_specs.append(pl.BlockSpec((rows, cols), lambda i, j: (j, 0)))
    return pl.pallas_call(
        kern,
        out_shape=tuple(out_shape),
        grid=(m // tm, steps),
        in_specs=in_specs,
        out_specs=tuple(out_specs),
        scratch_shapes=[pltpu.VMEM((tm, d), BF16)],
        compiler_params=_cparams(("arbitrary", "arbitrary")),
        name="in_proj",
    )(*args)


def _gate_kernel(g_ref, bias_ref, bt_ref, rt_ref, ct_ref, *, nh):
    row = lax.broadcasted_iota(jnp.int32, (CHUNK, CHUNK), 0)
    col = lax.broadcasted_iota(jnp.int32, (CHUNK, CHUNK), 1)
    tril = jnp.where(row >= col, 1.0, 0.0).astype(BF16)
    triu = jnp.where(row <= col, 1.0, 0.0).astype(BF16)
    is_ff = (col >= nh) & (col < 2 * nh)
    is_fb = (col >= 3 * nh) & (col < 4 * nh)
    grow = lax.broadcasted_iota(jnp.int32, (4 * nh, CHUNK), 0)
    tok = lax.broadcasted_iota(jnp.int32, (4 * nh, CHUNK), 1)

    def csum(tri, parts):
        out = jnp.dot(tri, parts[0], preferred_element_type=F32)
        for p in parts[1:]:
            out = out + jnp.dot(tri, p, preferred_element_type=F32)
        return out

    for c in range(g_ref.shape[0] // CHUNK):
        g = g_ref[c * CHUNK:(c + 1) * CHUNK, :] + bias_ref[...]
        lf = jnp.minimum(g, 0.0) - jnp.log(1.0 + jnp.exp(-jnp.abs(g)))
        hi = lf.astype(BF16)
        r1 = lf - hi.astype(F32)
        mid = r1.astype(BF16)
        lo = (r1 - mid.astype(F32)).astype(BF16)
        parts = (hi, mid, lo)
        bmat = jnp.where(is_ff, csum(tril, parts), jnp.where(is_fb, csum(triu, parts), 0.0))
        li = pltpu.roll(g, nh, axis=1)
        rmat = li - bmat
        bt_ref[c] = bmat.T[0:4 * nh, :] * LOG2E
        rt = rmat.T[0:4 * nh, :] * LOG2E
        rt_ref[c] = rt
        cf = rt
        cb = rt
        sh = 1
        while sh < CHUNK:
            cf = jnp.maximum(cf, jnp.where(tok >= sh, pltpu.roll(cf, sh, axis=1), NEG))
            cb = jnp.maximum(cb, jnp.where(tok < CHUNK - sh, pltpu.roll(cb, CHUNK - sh, axis=1), NEG))
            sh *= 2
        ct_ref[c] = jnp.where(grow < 2 * nh, cf, cb).astype(BF16).astype(F32)


def _gate_prep(g2d, bias, nh, rows):
    m = g2d.shape[0]
    nc = rows // CHUNK
    out = jax.ShapeDtypeStruct((m // CHUNK, 4 * nh, CHUNK), F32)
    ospec = pl.BlockSpec((nc, 4 * nh, CHUNK), lambda i: (i, 0, 0))
    return pl.pallas_call(
        functools.partial(_gate_kernel, nh=nh),
        out_shape=(out, out, out),
        grid=(m // rows,),
        in_specs=[pl.BlockSpec((rows, LANES), lambda i: (i, 0)),
                  pl.BlockSpec((1, LANES), lambda i: (0, 0))],
        out_specs=(ospec, ospec, ospec),
        compiler_params=_cparams(("arbitrary",)),
        name="gate_prep",
    )(g2d, bias)


def _mlstm_kernel(qx_ref, kx_ref, vx_ref, ox_ref, zx_ref, qc_ref, kc_ref, vc_ref, oc_ref, zc_ref,
                  btx_ref, rtx_ref, ctx_ref, btc_ref, rtc_ref, ctc_ref, gh_ref, yx_ref, yc_ref,
                  s_s, kv_s, sc_s, cp_s, mp_s, *, nh, group):
    h_idx = pl.program_id(1)
    row = lax.broadcasted_iota(jnp.int32, (CHUNK, CHUNK), 0)
    col = lax.broadcasted_iota(jnp.int32, (CHUNK, CHUNK), 1)
    masks = (row >= col, row <= col)
    ones_blk = jnp.ones((CHUNK, LANES), BF16)
    eye = row == col
    row2 = lax.broadcasted_iota(jnp.int32, (2 * CHUNK, 2 * LANES), 0)
    col2 = lax.broadcasted_iota(jnp.int32, (2 * CHUNK, 2 * LANES), 1)
    ones2 = jnp.where((row2 < CHUNK) == (col2 < LANES), 1.0, 0.0).astype(BF16)
    ncx = qx_ref.shape[0] // CHUNK
    ncc = qc_ref.shape[0] // CHUNK
    nct = ncc + ncx
    g_rows = (nh + h_idx, 3 * nh + h_idx)
    last = (CHUNK - 1, 0)

    trips = ncx // group
    ctx_per_trip = ncc // trips

    def both_segments(chunk_fn, latent_refs, context_refs):
        def body(i, _):
            for u in range(group):
                chunk_fn(i * group + u, ncc, *latent_refs)
            for u in range(ctx_per_trip):
                chunk_fn(i * ctx_per_trip + u, 0, *context_refs)
            return 0
        lax.fori_loop(0, trips, body, 0)

    def chunk_a(c, c0, q_ref, k_ref, v_ref, bt_ref, rt_ref, ct_ref):
        r0 = pl.multiple_of(c * CHUNK, CHUNK)
        rs = pl.multiple_of((c0 + c) * CHUNK, CHUNK)
        kt = k_ref[pl.ds(r0, CHUNK), :].astype(F32).T
        s_s[pl.ds(rs, CHUNK), :] = jnp.dot(q_ref[pl.ds(r0, CHUNK), :], kt.astype(BF16),
                                           preferred_element_type=F32)
        vaug = jnp.concatenate([v_ref[pl.ds(r0, CHUNK), :], ones_blk], axis=1)
        for d in range(2):
            b_last = bt_ref[c, pl.ds(g_rows[d], 1), last[d]:last[d] + 1]
            mg = b_last + ct_ref[c, pl.ds(g_rows[d], 1), last[d]:last[d] + 1]
            gl = b_last + rt_ref[c, pl.ds(g_rows[d], 1), :]
            kws = (kt * jnp.exp2(gl - mg)).astype(BF16)
            kv_s[d, c0 + c] = jnp.dot(kws, vaug, preferred_element_type=F32)
            sc_s[d, c0 + c, 0:1, :] = jnp.broadcast_to(b_last, (1, LANES))
            sc_s[d, c0 + c, 1:2, :] = jnp.broadcast_to(mg, (1, LANES))

    both_segments(chunk_a, (qx_ref, kx_ref, vx_ref, btx_ref, rtx_ref, ctx_ref),
                  (qc_ref, kc_ref, vc_ref, btc_ref, rtc_ref, ctc_ref))

    def scan(d):
        def body(j, carry):
            m_prev, c_prev = carry
            g = j if d == 0 else jnp.where(j < ncc, ncc - 1 - j, nct - 1 - j + ncc)
            cp_s[d, g] = c_prev.astype(BF16)
            mp_s[d, g] = jnp.broadcast_to(m_prev, (8, LANES))
            b_last = sc_s[d, g, 0:1, :]
            mg = sc_s[d, g, 1:2, :]
            m_new = jnp.maximum(b_last + m_prev, mg)
            a = jnp.exp2(b_last + m_prev - m_new)
            w = jnp.exp2(mg - m_new)
            c_new = (jnp.concatenate([a, a], axis=1) * c_prev
                     + jnp.concatenate([w, w], axis=1) * kv_s[d, g])
            return (m_new, c_new)
        lax.fori_loop(0, nct, body, (jnp.full((1, LANES), NEG, F32),
                                     jnp.zeros((LANES, 2 * LANES), F32)), unroll=2)

    scan(0)
    scan(1)

    def chunk_b(c, c0, q_ref, v_ref, o_ref, z_ref, bt_ref, rt_ref, ct_ref, y_ref):
        r0 = pl.multiple_of(c * CHUNK, CHUNK)
        rs = pl.multiple_of((c0 + c) * CHUNK, CHUNK)
        q = q_ref[pl.ds(r0, CHUNK), :].astype(F32)
        s = s_s[pl.ds(rs, CHUNK), :]
        vaug = jnp.concatenate([v_ref[pl.ds(r0, CHUNK), :], ones_blk], axis=1)
        h = None
        for d in range(2):
            c_row = ct_ref[c, pl.ds(g_rows[d], 1), :]
            r_row = rt_ref[c, pl.ds(g_rows[d], 1), :]
            m_prev = mp_s[d, c0 + c, 0:1, :]
            floor_row = jnp.exp2(-(bt_ref[c, pl.ds(g_rows[d], 1), :] + jnp.maximum(c_row, m_prev)))
            diag = jnp.concatenate([jnp.where(eye, c_row, 0.0), jnp.where(eye, floor_row, 0.0)],
                                   axis=1).astype(BF16)
            cols = jnp.dot(diag, ones2, preferred_element_type=F32)
            cm_bc = cols[:, 0:LANES]
            floor_bc = cols[:, LANES:]
            mm = jnp.maximum(cm_bc, m_prev)
            qw = (q * jnp.exp2(m_prev - mm)).astype(BF16)
            pm = (jnp.exp2(jnp.where(masks[d], r_row - mm, NEG)) * s).astype(BF16)
            nd = jnp.dot(jnp.concatenate([qw, pm], axis=1),
                         jnp.concatenate([cp_s[d, c0 + c], vaug], axis=0),
                         preferred_element_type=F32)
            hd = nd[:, 0:LANES] / jnp.maximum(jnp.abs(nd[:, LANES:]), floor_bc)
            h = hd if h is None else h + hd
        mu = jnp.mean(h, axis=1, keepdims=True)
        hc = h - mu
        hn = hc * lax.rsqrt(jnp.mean(hc * hc, axis=1, keepdims=True) + EPS) * gh_ref[...]
        z = z_ref[pl.ds(r0, CHUNK), :].astype(F32)
        o = o_ref[pl.ds(r0, CHUNK), :].astype(F32)
        y = hn * (z / ((1.0 + jnp.exp(-o)) * (1.0 + jnp.exp(-z))))
        y_ref[pl.ds(r0, CHUNK), :] = y.astype(BF16)

    both_segments(chunk_b, (qx_ref, vx_ref, ox_ref, zx_ref, btx_ref, rtx_ref, ctx_ref, yx_ref),
                  (qc_ref, vc_ref, oc_ref, zc_ref, btc_ref, rtc_ref, ctc_ref, yc_ref))


def _mlstm(px, pc, gates_x, gates_c, g_head, *, bsz, seq, ctx_len, nh, w_conv, w_ml):
    dh = LANES
    cb = 3 * w_conv // dh
    hb = w_ml // dh

    def colspec(rows, k):
        return pl.BlockSpec((rows, dh), lambda b, h: (b, cb + k * hb + h))

    def gspec(rows):
        return pl.BlockSpec((rows // CHUNK, 4 * nh, CHUNK), lambda b, h: (b, 0, 0))

    tot = seq + ctx_len
    nct = tot // CHUNK
    group = min(8, seq // CHUNK)
    assert (seq // CHUNK) % group == 0 and (ctx_len // CHUNK) % (seq // CHUNK // group) == 0
    return pl.pallas_call(
        functools.partial(_mlstm_kernel, nh=nh, group=group),
        out_shape=(jax.ShapeDtypeStruct((bsz * seq, w_ml), BF16),
                   jax.ShapeDtypeStruct((bsz * ctx_len, w_ml), BF16)),
        grid=(bsz, nh),
        in_specs=[colspec(seq, 0), colspec(seq, 1), colspec(seq, 2), colspec(seq, 3), colspec(seq, 4),
                  colspec(ctx_len, 0), colspec(ctx_len, 1), colspec(ctx_len, 2), colspec(ctx_len, 3),
                  colspec(ctx_len, 4),
                  gspec(seq), gspec(seq), gspec(seq), gspec(ctx_len), gspec(ctx_len), gspec(ctx_len),
                  pl.BlockSpec((1, dh), lambda b, h: (0, h))],
        out_specs=(pl.BlockSpec((seq, dh), lambda b, h: (b, h)),
                   pl.BlockSpec((ctx_len, dh), lambda b, h: (b, h))),
        scratch_shapes=[pltpu.VMEM((tot, CHUNK), F32),
                        pltpu.VMEM((2, nct, dh, 2 * dh), F32),
                        pltpu.VMEM((2, nct, 8, LANES), F32),
                        pltpu.VMEM((2, nct, dh, 2 * dh), BF16),
                        pltpu.VMEM((2, nct, 8, LANES), F32)],
        compiler_params=_cparams(("arbitrary", "arbitrary")),
        name="mlstm",
    )(px, px, px, px, px, pc, pc, pc, pc, pc, *gates_x, *gates_c, g_head)


def _glu(a_ref, g_ref, r0, rows):
    a = a_ref[r0:r0 + rows, :].astype(F32)
    g = g_ref[r0:r0 + rows, :].astype(F32)
    return a * (0.5 * jnp.tanh(0.5 * g) + 0.5)


def _segconv_kernel(a_ref, g_ref, w_ref, b_ref, o_ref, pad_ref, *, seg, sub):
    rows, cw = a_ref.shape
    stride = seg + 2 * SEG_PAD
    total = rows // seg * stride
    zeros = jnp.zeros((SEG_PAD, cw), F32)
    for s in range(rows // seg):
        pad_ref[0, s * stride:s * stride + SEG_PAD, :] = zeros
        pad_ref[0, s * stride + SEG_PAD:s * stride + SEG_PAD + seg, :] = _glu(a_ref, g_ref, s * seg, seg)
        pad_ref[0, s * stride + SEG_PAD + seg:(s + 1) * stride, :] = zeros
    pad_ref[0, total:total + SUBLANES, :] = jnp.zeros((SUBLANES, cw), F32)
    for r in range(1, SUBLANES):
        pad_ref[r, 0:total, :] = pad_ref[0, r:r + total, :]
    for s in range(rows // seg):
        for t0 in range(0, seg, sub):
            start = s * stride + SEG_PAD + t0 - CONV_PAD
            acc = jnp.zeros((sub, cw), F32) + b_ref[...]
            for k in range(CONV_K):
                r = (start + k) % SUBLANES
                acc = acc + pad_ref[r, start + k - r:start + k - r + sub, :] * w_ref[k:k + 1, :]
            o_ref[s * seg + t0:s * seg + t0 + sub, :] = acc


def _segconv(p, w_dw, b_dw, *, seg, rows, cw, ncol, a_col0, g_col0, w_col0):
    m = p.shape[0]
    stride = seg + 2 * SEG_PAD
    return pl.pallas_call(
        functools.partial(_segconv_kernel, seg=seg, sub=min(seg, 64)),
        out_shape=jax.ShapeDtypeStruct((m, ncol * cw), F32),
        grid=(m // rows, ncol),
        in_specs=[pl.BlockSpec((rows, cw), lambda i, j: (i, a_col0 + j)),
                  pl.BlockSpec((rows, cw), lambda i, j: (i, g_col0 + j)),
                  pl.BlockSpec((CONV_K, cw), lambda i, j: (0, w_col0 + j)),
                  pl.BlockSpec((1, cw), lambda i, j: (0, w_col0 + j))],
        out_specs=pl.BlockSpec((rows, cw), lambda i, j: (i, j)),
        scratch_shapes=[pltpu.VMEM((SUBLANES, rows // seg * stride + SUBLANES, cw), F32)],
        compiler_params=_cparams(("arbitrary", "arbitrary")),
        name="segconv",
    )(p, p, w_dw, b_dw)


def _rowconv_kernel(a_ref, g_ref, w_ref, b_ref, o_ref, pad_ref, *, nrows):
    cw = a_ref.shape[1]
    edge = CONV_PAD * GRID_W
    pad_ref[0:edge, :] = jnp.zeros((edge, cw), F32)
    pad_ref[edge + nrows * GRID_W:2 * edge + nrows * GRID_W, :] = jnp.zeros((edge, cw), F32)
    for r in range(nrows):
        pad_ref[edge + r * GRID_W:edge + (r + 1) * GRID_W, :] = _glu(a_ref, g_ref, r * GRID_W, GRID_W)

    def body(r, _):
        acc = jnp.zeros((GRID_W, cw), F32) + b_ref[...]
        for k in range(CONV_K):
            src = pl.multiple_of((r + k) * GRID_W, GRID_W)
            acc = acc + pad_ref[pl.ds(src, GRID_W), :] * w_ref[k:k + 1, :]
        o_ref[pl.ds(pl.multiple_of(r * GRID_W, GRID_W), GRID_W), :] = acc
        return 0
    lax.fori_loop(0, nrows, body, 0)


def _rowconv(p, w_dw, b_dw, *, seq, cw, ncol, a_col0, g_col0, w_col0):
    m = p.shape[0]
    nrows = seq // GRID_W
    return pl.pallas_call(
        functools.partial(_rowconv_kernel, nrows=nrows),
        out_shape=jax.ShapeDtypeStruct((m, ncol * cw), F32),
        grid=(m // seq, ncol),
        in_specs=[pl.BlockSpec((seq, cw), lambda i, j: (i, a_col0 + j)),
                  pl.BlockSpec((seq, cw), lambda i, j: (i, g_col0 + j)),
                  pl.BlockSpec((CONV_K, cw), lambda i, j: (0, w_col0 + j)),
                  pl.BlockSpec((1, cw), lambda i, j: (0, w_col0 + j))],
        out_specs=pl.BlockSpec((seq, cw), lambda i, j: (i, j)),
        scratch_shapes=[pltpu.VMEM((seq + 2 * CONV_PAD * GRID_W, cw), F32)],
        compiler_params=_cparams(("arbitrary", "arbitrary")),
        name="rowconv",
    )(p, p, w_dw, b_dw)


def _out_kernel(cva_ref, cvb_ref, z_ref, ym_ref, x_ref, gt_ref, lng_ref, lnb_ref, gpost_ref,
                wpw_ref, wout_ref, o_ref):
    w_conv = wpw_ref.shape[0]
    u = jnp.concatenate([cva_ref[...], cvb_ref[...]], axis=1)
    mu = jnp.mean(u, axis=-1, keepdims=True)
    uc = u - mu
    r = uc * lax.rsqrt(jnp.mean(uc * uc, axis=-1, keepdims=True) + EPS) * lng_ref[...] + lnb_ref[...]
    t = jnp.dot((r * _sigmoid(r)).astype(BF16), wpw_ref[...], preferred_element_type=F32)
    z = z_ref[...].astype(F32)
    yc = (t * (z * _sigmoid(z))).astype(BF16)
    out = (jnp.dot(yc, wout_ref[0:w_conv, :], preferred_element_type=F32)
           + jnp.dot(ym_ref[...], wout_ref[w_conv:, :], preferred_element_type=F32))
    ms = jnp.mean(out * out, axis=-1, keepdims=True)
    o_ref[...] = x_ref[...] + (out * lax.rsqrt(ms + EPS)) * (gt_ref[0] * gpost_ref[...])


def _out_proj(cva, cvb, cvb_col, p, ym, x2d, ada3, row_of_tile, ln_g, ln_b, g_post, w_pw2, w_out,
              *, tm, w_conv):
    m, d = x2d.shape
    half = w_conv // 2
    w_mix = w_out.shape[0]
    once = pl.Buffered(1)
    return pl.pallas_call(
        _out_kernel,
        out_shape=jax.ShapeDtypeStruct((m, d), F32),
        grid=(m // tm,),
        in_specs=[pl.BlockSpec((tm, half), lambda i: (i, 0)),
                  pl.BlockSpec((tm, half), lambda i: (i, cvb_col)),
                  pl.BlockSpec((tm, w_conv), lambda i: (i, 2)),
                  pl.BlockSpec((tm, w_mix - w_conv), lambda i: (i, 0)),
                  pl.BlockSpec((tm, d), lambda i: (i, 0)),
                  pl.BlockSpec((1, 1, d), lambda i: (row_of_tile(i) * 3 + 2, 0, 0)),
                  pl.BlockSpec((1, w_conv), lambda i: (0, 0)),
                  pl.BlockSpec((1, w_conv), lambda i: (0, 0)),
                  pl.BlockSpec((1, d), lambda i: (0, 0)),
                  pl.BlockSpec((w_conv, w_conv), lambda i: (0, 0), pipeline_mode=once),
                  pl.BlockSpec((w_mix, d), lambda i: (0, 0), pipeline_mode=once)],
        out_specs=pl.BlockSpec((tm, d), lambda i: (i, 0)),
        compiler_params=_cparams(("arbitrary",)),
        name="out_proj",
    )(cva, cvb, p, ym, x2d, ada3, ln_g, ln_b, g_post, w_pw2, w_out)


def kernel(x, c, ctx, c_ctx, w_ada, b_ada, g_pre, g_post, w_in, b_gate, w_dw, b_dw, ln_g, ln_b, w_pw2,
           g_head, w_out):
    bsz, seq, d = x.shape
    ctx_len = ctx.shape[1]
    depth = w_ada.shape[0]
    w_conv = w_dw.shape[-1]
    w_ml = g_head.shape[-1]
    nh = b_gate.shape[-1] // 4
    n_main = 3 * w_conv + 5 * w_ml
    half = w_conv // 2
    assert w_ml == nh * LANES and 4 * nh <= LANES and w_conv == w_ml
    assert seq % CHUNK == 0 and ctx_len % CHUNK == 0 and seq % GRID_W == 0 and half % LANES == 0

    ctx_row = bsz
    cond = jnp.concatenate([c, c_ctx[None, :], jnp.zeros((8 - bsz - 1, d), F32)], axis=0)
    ada = _ada_all_layers(cond, w_ada, b_ada)
    ada3 = ada.reshape(depth * 8 * 3, 1, d)

    w_in_t = jnp.swapaxes(w_in, 1, 2)
    bias_g = jnp.pad(b_gate, ((0, 0), (0, LANES - 4 * nh)))

    tm_x = min(1024, seq)
    tm_c = min(1024, bsz * ctx_len)
    tn = min(1024, w_conv)
    tm_o = min(512, seq)
    tm_oc = min(256, ctx_len)
    cw = min(256, half)

    xs = x.reshape(bsz * seq, d)
    cs = ctx.reshape(bsz * ctx_len, d)
    for l in range(depth):
        last = l == depth - 1
        row_x = lambda i, l=l, t=seq // tm_x: l * 8 + i // t
        row_c = lambda i, l=l: l * 8 + ctx_row + 0 * i
        pc, gc, w_main_t, w_pw2_b, w_out_b = _in_proj(
            cs, ada3, row_c, g_pre[l][None], None, w_in_t, tm=tm_c, tn=tn // 2, w_conv=w_conv,
            w_ml=w_ml, nh=nh, layer=l, cast_also=(w_pw2, w_out))
        px, gx = _in_proj(xs, ada3, row_x, g_pre[l][None], w_main_t, w_in_t,
                          tm=tm_x, tn=2 * tn, w_conv=w_conv, w_ml=w_ml, nh=nh, layer=l)
        gates_x = _gate_prep(gx, bias_g[l][None], nh, min(1024, seq))
        gates_c = _gate_prep(gc, bias_g[l][None], nh, min(1024, ctx_len))
        ymx, ymc = _mlstm(px, pc, gates_x, gates_c, g_head[l][None], bsz=bsz, seq=seq,
                          ctx_len=ctx_len, nh=nh, w_conv=w_conv, w_ml=w_ml)
        ncol = half // cw
        g0 = w_conv // cw
        cv_w = _segconv(px, w_dw[l], b_dw[l][None], seg=GRID_W, rows=min(512, seq), cw=cw, ncol=ncol,
                        a_col0=0, g_col0=g0, w_col0=0)
        cv_h = _rowconv(px, w_dw[l], b_dw[l][None], seq=seq, cw=cw, ncol=ncol,
                        a_col0=ncol, g_col0=g0 + ncol, w_col0=ncol)
        row_xo = lambda i, l=l, t=seq // tm_o: l * 8 + i // t
        xs_new = _out_proj(cv_w, cv_h, 0, px, ymx, xs, ada3, row_xo, ln_g[l][None], ln_b[l][None],
                           g_post[l][None], w_pw2_b, w_out_b, tm=tm_o, w_conv=w_conv)
        if not last:
            cv_c = _segconv(pc, w_dw[l], b_dw[l][None], seg=ctx_len, rows=ctx_len, cw=cw,
                            ncol=2 * ncol, a_col0=0, g_col0=g0, w_col0=0)
            cs = _out_proj(cv_c, cv_c, 1, pc, ymc, cs, ada3, row_c, ln_g[l][None], ln_b[l][None],
                           g_post[l][None], w_pw2_b, w_out_b, tm=tm_oc, w_conv=w_conv)
        xs = xs_new
    return xs.reshape(bsz, seq, d)
```

```python
import functools

import jax
import jax.numpy as jnp
from jax import lax
from jax.experimental import pallas as pl
from jax.experimental.pallas import tpu as pltpu

F32 = jnp.float32
BF16 = jnp.bfloat16

GRID_W = 64
CONV_K = 31
CONV_PAD = CONV_K // 2
EPS = 1e-6
NEG = -1e30
LOG2E = 1.4426950408889634
LANES = 128
SUBLANES = 8
CHUNK = 128
SEG_PAD = 16
VMEM_LIMIT = 56 * 1024 * 1024


def _cparams(sem):
    return pltpu.CompilerParams(dimension_semantics=sem, vmem_limit_bytes=VMEM_LIMIT)


def _sigmoid(x):
    return 1.0 / (1.0 + jnp.exp(-x))


def _ada_kernel(c_ref, w_ref, b_ref, o_ref):
    c = c_ref[...]
    s = (c * _sigmoid(c)).astype(BF16)
    o_ref[0] = jnp.dot(s, w_ref[0].astype(BF16), preferred_element_type=F32) + b_ref[0]


def _ada_all_layers(cond, w_ada, b_ada):
    depth, d, n = w_ada.shape
    tn = 1024 if n % 1024 == 0 else n // 3
    return pl.pallas_call(
        _ada_kernel,
        out_shape=jax.ShapeDtypeStruct((depth, 8, n), F32),
        grid=(depth, n // tn),
        in_specs=[pl.BlockSpec((8, d), lambda l, j: (0, 0)),
                  pl.BlockSpec((1, d, tn), lambda l, j: (l, 0, j)),
                  pl.BlockSpec((1, 1, tn), lambda l, j: (l, 0, j))],
        out_specs=pl.BlockSpec((1, 8, tn), lambda l, j: (l, 0, j)),
        compiler_params=_cparams(("arbitrary", "arbitrary")),
        name="ada",
    )(cond, w_ada, b_ada.reshape(depth, 1, n))


_NT = (((1,), (1,)), ((), ()))


def _in_kernel(x_ref, sh_ref, sc_ref, gpre_ref, w_ref, wg_ref, *rest,
               tn, q_lo, q_hi, q_scale, sub, emit_w):
    if emit_w:
        wpw32_ref, wout32_ref, p_ref, g_ref, wb_ref, wpwb_ref, woutb_ref, hx_ref = rest
        wpwb_ref[...] = wpw32_ref[...].astype(BF16)
        woutb_ref[...] = wout32_ref[...].astype(BF16)
    else:
        p_ref, g_ref, hx_ref = rest
    j = pl.program_id(1)

    @pl.when(j == 0)
    def _():
        tm, d = x_ref.shape
        gain = gpre_ref[...] * (1.0 + sc_ref[0])
        for r in range(0, tm, sub):
            x = x_ref[r:r + sub, :]
            ms = jnp.mean(x * x, axis=-1, keepdims=True)
            h = (x * lax.rsqrt(ms + EPS)) * gain + sh_ref[0]
            hx_ref[r:r + sub, :] = h.astype(BF16)
        wg = wg_ref[...].astype(BF16)
        wg = jnp.concatenate([wg, jnp.zeros((LANES - wg.shape[0], d), BF16)], axis=0)
        g_ref[...] = lax.dot_general(hx_ref[...], wg, _NT, preferred_element_type=F32)

    w = w_ref[...]
    if emit_w:
        w = w.astype(BF16)
        wb_ref[...] = w
    acc = lax.dot_general(hx_ref[...], w, _NT, preferred_element_type=F32)
    col = j * tn + lax.broadcasted_iota(jnp.int32, (1, tn), 1)
    scale = jnp.where((col >= q_lo) & (col < q_hi), q_scale, 1.0).astype(F32)
    p_ref[...] = (acc * scale).astype(BF16)


def _in_proj(x2d, ada3, row_of_tile, g_pre, w_t, w_in_t, *, tm, tn, w_conv, w_ml, nh, layer,
             cast_also=None):
    m, d = x2d.shape
    n = 3 * w_conv + 5 * w_ml
    q_lo = 3 * w_conv
    steps = n // tn
    emit_w = cast_also is not None
    kern = functools.partial(_in_kernel, tn=tn, q_lo=q_lo, q_hi=q_lo + w_ml,
                             q_scale=float(LANES) ** -0.5, sub=min(tm, 256), emit_w=emit_w)
    in_specs = [pl.BlockSpec((tm, d), lambda i, j: (i, 0)),
                pl.BlockSpec((1, 1, d), lambda i, j: (row_of_tile(i) * 3 + 0, 0, 0)),
                pl.BlockSpec((1, 1, d), lambda i, j: (row_of_tile(i) * 3 + 1, 0, 0)),
                pl.BlockSpec((1, d), lambda i, j: (0, 0)),
                pl.BlockSpec((tn, d), lambda i, j: (j, 0)),
                pl.BlockSpec((None, 4 * nh, d), lambda i, j: (layer, n // (4 * nh), 0))]
    args = [x2d, ada3, ada3, g_pre, w_t, w_in_t]
    out_shape = [jax.ShapeDtypeStruct((m, n), BF16), jax.ShapeDtypeStruct((m, LANES), F32)]
    out_specs = [pl.BlockSpec((tm, tn), lambda i, j: (i, j)),
                 pl.BlockSpec((tm, LANES), lambda i, j: (i, 0))]
    if emit_w:
        assert m == tm
        in_specs[4] = pl.BlockSpec((None, tn, d), lambda i, j: (layer, j, 0))
        args[4] = w_in_t
        out_shape.append(jax.ShapeDtypeStruct((n, d), BF16))
        out_specs.append(pl.BlockSpec((tn, d), lambda i, j: (j, 0)))
        for w32 in cast_also:
            rows, cols = w32.shape[1] // steps, w32.shape[2]
            in_specs.append(pl.BlockSpec((None, rows, cols), lambda i, j: (layer, j, 0)))
            args.append(w32)
            out_shape.append(jax.ShapeDtypeStruct(w32.shape[1:], BF16))
            out_specs.append(pl.BlockSpec((rows, cols), lambda i, j: (j, 0)))
    return pl.pallas_call(
        kern,
        out_shape=tuple(out_shape),
        grid=(m // tm, steps),
        in_specs=in_specs,
        out_specs=tuple(out_specs),
        scratch_shapes=[pltpu.VMEM((tm, d), BF16)],
        compiler_params=_cparams(("arbitrary", "arbitrary")),
        name="in_proj",
    )(*args)


def _gate_kernel(g_ref, bias_ref, bt_ref, rt_ref, ct_ref, *, nh):
    row = lax.broadcasted_iota(jnp.int32, (CHUNK, CHUNK), 0)
    col = lax.broadcasted_iota(jnp.int32, (CHUNK, CHUNK), 1)
    tril = jnp.where(row >= col, 1.0, 0.0).astype(BF16)
    is_ff = (col >= nh) & (col < 2 * nh)
    is_fb = (col >= 3 * nh) & (col < 4 * nh)
    grow = lax.broadcasted_iota(jnp.int32, (4 * nh, CHUNK), 0)
    tok = lax.broadcasted_iota(jnp.int32, (4 * nh, CHUNK), 1)

    def csum(tri, parts):
        out = jnp.dot(tri, parts[0], preferred_element_type=F32)
        for p in parts[1:]:
            out = out + jnp.dot(tri, p, preferred_element_type=F32)
        return out

    for c in range(g_ref.shape[0] // CHUNK):
        g = g_ref[c * CHUNK:(c + 1) * CHUNK, :] + bias_ref[...]
        lf = jnp.minimum(g, 0.0) - jnp.log(1.0 + jnp.exp(-jnp.abs(g)))
        hi = lf.astype(BF16)
        r1 = lf - hi.astype(F32)
        mid = r1.astype(BF16)
        lo = (r1 - mid.astype(F32)).astype(BF16)
        pre = csum(tril, (hi, mid, lo))
        suf = pre[CHUNK - 1:CHUNK, :] - pre + lf
        bmat = jnp.where(is_ff, pre, jnp.where(is_fb, suf, 0.0))
        li = pltpu.roll(g, nh, axis=1)
        rmat = li - bmat
        bt_ref[c] = bmat.T[0:4 * nh, :] * LOG2E
        rt = rmat.T[0:4 * nh, :] * LOG2E
        rt_ref[c] = rt
        cf = rt
        cb = rt
        sh = 1
        while sh < CHUNK:
            cf = jnp.maximum(cf, jnp.where(tok >= sh, pltpu.roll(cf, sh, axis=1), NEG))
            cb = jnp.maximum(cb, jnp.where(tok < CHUNK - sh, pltpu.roll(cb, CHUNK - sh, axis=1), NEG))
            sh *= 2
        ct_ref[c] = jnp.where(grow < 2 * nh, cf, cb).astype(BF16).astype(F32)


def _gate_prep(g2d, bias, nh, rows):
    m = g2d.shape[0]
    nc = rows // CHUNK
    out = jax.ShapeDtypeStruct((m // CHUNK, 4 * nh, CHUNK), F32)
    ospec = pl.BlockSpec((nc, 4 * nh, CHUNK), lambda i: (i, 0, 0))
    return pl.pallas_call(
        functools.partial(_gate_kernel, nh=nh),
        out_shape=(out, out, out),
        grid=(m // rows,),
        in_specs=[pl.BlockSpec((rows, LANES), lambda i: (i, 0)),
                  pl.BlockSpec((1, LANES), lambda i: (0, 0))],
        out_specs=(ospec, ospec, ospec),
        compiler_params=_cparams(("arbitrary",)),
        name="gate_prep",
    )(g2d, bias)


def _mlstm_kernel(qx_ref, kx_ref, vx_ref, ox_ref, zx_ref, qc_ref, kc_ref, vc_ref, oc_ref, zc_ref,
                  btx_ref, rtx_ref, ctx_ref, btc_ref, rtc_ref, ctc_ref, gh_ref, yx_ref, yc_ref,
                  s_s, kv_s, sc_s, cp_s, mp_s, *, nh, group):
    h_idx = pl.program_id(1)
    row = lax.broadcasted_iota(jnp.int32, (CHUNK, CHUNK), 0)
    col = lax.broadcasted_iota(jnp.int32, (CHUNK, CHUNK), 1)
    masks = (row >= col, row <= col)
    ones_blk = jnp.ones((CHUNK, LANES), BF16)
    eye = row == col
    row2 = lax.broadcasted_iota(jnp.int32, (2 * CHUNK, 2 * LANES), 0)
    col2 = lax.broadcasted_iota(jnp.int32, (2 * CHUNK, 2 * LANES), 1)
    ones2 = jnp.where((row2 < CHUNK) == (col2 < LANES), 1.0, 0.0).astype(BF16)
    ncx = qx_ref.shape[0] // CHUNK
    ncc = qc_ref.shape[0] // CHUNK
    nct = ncc + ncx
    g_rows = (nh + h_idx, 3 * nh + h_idx)
    last = (CHUNK - 1, 0)

    trips = ncx // group
    ctx_per_trip = ncc // trips

    def both_segments(chunk_fn, latent_refs, context_refs):
        def body(i, _):
            for u in range(group):
                chunk_fn(i * group + u, ncc, *latent_refs)
            for u in range(ctx_per_trip):
                chunk_fn(i * ctx_per_trip + u, 0, *context_refs)
            return 0
        lax.fori_loop(0, trips, body, 0)

    def chunk_a(c, c0, q_ref, k_ref, v_ref, bt_ref, rt_ref, ct_ref):
        r0 = pl.multiple_of(c * CHUNK, CHUNK)
        rs = pl.multiple_of((c0 + c) * CHUNK, CHUNK)
        kt = k_ref[pl.ds(r0, CHUNK), :].astype(F32).T
        s_s[pl.ds(rs, CHUNK), :] = jnp.dot(q_ref[pl.ds(r0, CHUNK), :], kt.astype(BF16),
                                           preferred_element_type=F32)
        vaug = jnp.concatenate([v_ref[pl.ds(r0, CHUNK), :], ones_blk], axis=1)
        for d in range(2):
            b_last = bt_ref[c, pl.ds(g_rows[d], 1), last[d]:last[d] + 1]
            mg = b_last + ct_ref[c, pl.ds(g_rows[d], 1), last[d]:last[d] + 1]
            gl = b_last + rt_ref[c, pl.ds(g_rows[d], 1), :]
            kws = (kt * jnp.exp2(gl - mg)).astype(BF16)
            kv_s[d, c0 + c] = jnp.dot(kws, vaug, preferred_element_type=F32)
            sc_s[d, c0 + c, 0:1, :] = jnp.broadcast_to(b_last, (1, LANES))
            sc_s[d, c0 + c, 1:2, :] = jnp.broadcast_to(mg, (1, LANES))

    both_segments(chunk_a, (qx_ref, kx_ref, vx_ref, btx_ref, rtx_ref, ctx_ref),
                  (qc_ref, kc_ref, vc_ref, btc_ref, rtc_ref, ctc_ref))

    def scan(d):
        def body(j, carry):
            m_prev, c_prev = carry
            g = j if d == 0 else jnp.where(j < ncc, ncc - 1 - j, nct - 1 - j + ncc)
            cp_s[d, g] = c_prev.astype(BF16)
            mp_s[d, g] = jnp.broadcast_to(m_prev, (8, LANES))
            b_last = sc_s[d, g, 0:1, :]
            mg = sc_s[d, g, 1:2, :]
            m_new = jnp.maximum(b_last + m_prev, mg)
            a = jnp.exp2(b_last + m_prev - m_new)
            w = jnp.exp2(mg - m_new)
            c_new = (jnp.concatenate([a, a], axis=1) * c_prev
                     + jnp.concatenate([w, w], axis=1) * kv_s[d, g])
            return (m_new, c_new)
        lax.fori_loop(0, nct, body, (jnp.full((1, LANES), NEG, F32),
                                     jnp.zeros((LANES, 2 * LANES), F32)), unroll=2)

    scan(0)
    scan(1)

    def chunk_b(c, c0, q_ref, v_ref, o_ref, z_ref, bt_ref, rt_ref, ct_ref, y_ref):
        r0 = pl.multiple_of(c * CHUNK, CHUNK)
        rs = pl.multiple_of((c0 + c) * CHUNK, CHUNK)
        q = q_ref[pl.ds(r0, CHUNK), :].astype(F32)
        s = s_s[pl.ds(rs, CHUNK), :]
        vaug = jnp.concatenate([v_ref[pl.ds(r0, CHUNK), :], ones_blk], axis=1)
        h = None
        for d in range(2):
            c_row = ct_ref[c, pl.ds(g_rows[d], 1), :]
            r_row = rt_ref[c, pl.ds(g_rows[d], 1), :]
            m_prev = mp_s[d, c0 + c, 0:1, :]
            floor_row = jnp.exp2(-(bt_ref[c, pl.ds(g_rows[d], 1), :] + jnp.maximum(c_row, m_prev)))
            diag = jnp.concatenate([jnp.where(eye, c_row, 0.0), jnp.where(eye, floor_row, 0.0)],
                                   axis=1).astype(BF16)
            cols = jnp.dot(diag, ones2, preferred_element_type=F32)
            cm_bc = cols[:, 0:LANES]
            floor_bc = cols[:, LANES:]
            mm = jnp.maximum(cm_bc, m_prev)
            qw = (q * jnp.exp2(m_prev - mm)).astype(BF16)
            pm = (jnp.exp2(jnp.where(masks[d], r_row - mm, NEG)) * s).astype(BF16)
            nd = jnp.dot(jnp.concatenate([qw, pm], axis=1),
                         jnp.concatenate([cp_s[d, c0 + c], vaug], axis=0),
                         preferred_element_type=F32)
            hd = nd[:, 0:LANES] / jnp.maximum(jnp.abs(nd[:, LANES:]), floor_bc)
            h = hd if h is None else h + hd
        mu = jnp.mean(h, axis=1, keepdims=True)
        hc = h - mu
        hn = hc * lax.rsqrt(jnp.mean(hc * hc, axis=1, keepdims=True) + EPS) * gh_ref[...]
        z = z_ref[pl.ds(r0, CHUNK), :].astype(F32)
        o = o_ref[pl.ds(r0, CHUNK), :].astype(F32)
        y = hn * (z / ((1.0 + jnp.exp(-o)) * (1.0 + jnp.exp(-z))))
        y_ref[pl.ds(r0, CHUNK), :] = y.astype(BF16)

    both_segments(chunk_b, (qx_ref, vx_ref, ox_ref, zx_ref, btx_ref, rtx_ref, ctx_ref, yx_ref),
                  (qc_ref, vc_ref, oc_ref, zc_ref, btc_ref, rtc_ref, ctc_ref, yc_ref))


def _mlstm(px, pc, gates_x, gates_c, g_head, *, bsz, seq, ctx_len, nh, w_conv, w_ml):
    dh = LANES
    cb = 3 * w_conv // dh
    hb = w_ml // dh

    def colspec(rows, k):
        return pl.BlockSpec((rows, dh), lambda b, h: (b, cb + k * hb + h))

    def gspec(rows):
        return pl.BlockSpec((rows // CHUNK, 4 * nh, CHUNK), lambda b, h: (b, 0, 0))

    tot = seq + ctx_len
    nct = tot // CHUNK
    group = min(8, seq // CHUNK)
    assert (seq // CHUNK) % group == 0 and (ctx_len // CHUNK) % (seq // CHUNK // group) == 0
    return pl.pallas_call(
        functools.partial(_mlstm_kernel, nh=nh, group=group),
        out_shape=(jax.ShapeDtypeStruct((bsz * seq, w_ml), BF16),
                   jax.ShapeDtypeStruct((bsz * ctx_len, w_ml), BF16)),
        grid=(bsz, nh),
        in_specs=[colspec(seq, 0), colspec(seq, 1), colspec(seq, 2), colspec(seq, 3), colspec(seq, 4),
                  colspec(ctx_len, 0), colspec(ctx_len, 1), colspec(ctx_len, 2), colspec(ctx_len, 3),
                  colspec(ctx_len, 4),
                  gspec(seq), gspec(seq), gspec(seq), gspec(ctx_len), gspec(ctx_len), gspec(ctx_len),
                  pl.BlockSpec((1, dh), lambda b, h: (0, h))],
        out_specs=(pl.BlockSpec((seq, dh), lambda b, h: (b, h)),
                   pl.BlockSpec((ctx_len, dh), lambda b, h: (b, h))),
        scratch_shapes=[pltpu.VMEM((tot, CHUNK), F32),
                        pltpu.VMEM((2, nct, dh, 2 * dh), F32),
                        pltpu.VMEM((2, nct, 8, LANES), F32),
                        pltpu.VMEM((2, nct, dh, 2 * dh), BF16),
                        pltpu.VMEM((2, nct, 8, LANES), F32)],
        compiler_params=_cparams(("arbitrary", "arbitrary")),
        name="mlstm",
    )(px, px, px, px, px, pc, pc, pc, pc, pc, *gates_x, *gates_c, g_head)


def _glu(a_ref, g_ref, r0, rows):
    a = a_ref[r0:r0 + rows, :].astype(F32)
    g = g_ref[r0:r0 + rows, :].astype(F32)
    return a * (0.5 * jnp.tanh(0.5 * g) + 0.5)


def _segconv_kernel(a_ref, g_ref, w_ref, b_ref, o_ref, pad_ref, *, seg, sub):
    rows, cw = a_ref.shape
    stride = seg + 2 * SEG_PAD
    total = rows // seg * stride
    zeros = jnp.zeros((SEG_PAD, cw), F32)
    for s in range(rows // seg):
        pad_ref[0, s * stride:s * stride + SEG_PAD, :] = zeros
        pad_ref[0, s * stride + SEG_PAD:s * stride + SEG_PAD + seg, :] = _glu(a_ref, g_ref, s * seg, seg)
        pad_ref[0, s * stride + SEG_PAD + seg:(s + 1) * stride, :] = zeros
    pad_ref[0, total:total + SUBLANES, :] = jnp.zeros((SUBLANES, cw), F32)
    for r in range(1, SUBLANES):
        pad_ref[r, 0:total, :] = pad_ref[0, r:r + total, :]
    for s in range(rows // seg):
        for t0 in range(0, seg, sub):
            start = s * stride + SEG_PAD + t0 - CONV_PAD
            acc = jnp.zeros((sub, cw), F32) + b_ref[...]
            for k in range(CONV_K):
                r = (start + k) % SUBLANES
                acc = acc + pad_ref[r, start + k - r:start + k - r + sub, :] * w_ref[k:k + 1, :]
            o_ref[s * seg + t0:s * seg + t0 + sub, :] = acc


def _segconv(p, w_dw, b_dw, *, seg, rows, cw, ncol, a_col0, g_col0, w_col0):
    m = p.shape[0]
    stride = seg + 2 * SEG_PAD
    return pl.pallas_call(
        functools.partial(_segconv_kernel, seg=seg, sub=min(seg, 64)),
        out_shape=jax.ShapeDtypeStruct((m, ncol * cw), F32),
        grid=(m // rows, ncol),
        in_specs=[pl.BlockSpec((rows, cw), lambda i, j: (i, a_col0 + j)),
                  pl.BlockSpec((rows, cw), lambda i, j: (i, g_col0 + j)),
                  pl.BlockSpec((CONV_K, cw), lambda i, j: (0, w_col0 + j)),
                  pl.BlockSpec((1, cw), lambda i, j: (0, w_col0 + j))],
        out_specs=pl.BlockSpec((rows, cw), lambda i, j: (i, j)),
        scratch_shapes=[pltpu.VMEM((SUBLANES, rows // seg * stride + SUBLANES, cw), F32)],
        compiler_params=_cparams(("arbitrary", "arbitrary")),
        name="segconv",
    )(p, p, w_dw, b_dw)


def _rowconv_kernel(a_ref, g_ref, w_ref, b_ref, o_ref, u_ref, *, nrows):
    cw = a_ref.shape[1]
    for r in range(nrows):
        u_ref[r * GRID_W:(r + 1) * GRID_W, :] = _glu(a_ref, g_ref, r * GRID_W, GRID_W)
    for r in range(nrows):
        acc = jnp.zeros((GRID_W, cw), F32) + b_ref[...]
        for k in range(CONV_K):
            src = r + k - CONV_PAD
            if 0 <= src < nrows:
                acc = acc + u_ref[src * GRID_W:(src + 1) * GRID_W, :] * w_ref[k:k + 1, :]
        o_ref[r * GRID_W:(r + 1) * GRID_W, :] = acc


def _rowconv(p, w_dw, b_dw, *, seq, cw, ncol, a_col0, g_col0, w_col0):
    m = p.shape[0]
    nrows = seq // GRID_W
    return pl.pallas_call(
        functools.partial(_rowconv_kernel, nrows=nrows),
        out_shape=jax.ShapeDtypeStruct((m, ncol * cw), F32),
        grid=(m // seq, ncol),
        in_specs=[pl.BlockSpec((seq, cw), lambda i, j: (i, a_col0 + j)),
                  pl.BlockSpec((seq, cw), lambda i, j: (i, g_col0 + j)),
                  pl.BlockSpec((CONV_K, cw), lambda i, j: (0, w_col0 + j)),
                  pl.BlockSpec((1, cw), lambda i, j: (0, w_col0 + j))],
        out_specs=pl.BlockSpec((seq, cw), lambda i, j: (i, j)),
        scratch_shapes=[pltpu.VMEM((seq, cw), F32)],
        compiler_params=_cparams(("arbitrary", "arbitrary")),
        name="rowconv",
    )(p, p, w_dw, b_dw)


def _out_kernel(cva_ref, cvb_ref, z_ref, ym_ref, x_ref, gt_ref, lng_ref, lnb_ref, gpost_ref,
                wpw_ref, wout_ref, o_ref):
    w_conv = wpw_ref.shape[0]
    u = jnp.concatenate([cva_ref[...], cvb_ref[...]], axis=1)
    mu = jnp.mean(u, axis=-1, keepdims=True)
    uc = u - mu
    r = uc * lax.rsqrt(jnp.mean(uc * uc, axis=-1, keepdims=True) + EPS) * lng_ref[...] + lnb_ref[...]
    t = jnp.dot((r * _sigmoid(r)).astype(BF16), wpw_ref[...], preferred_element_type=F32)
    z = z_ref[...].astype(F32)
    yc = (t * (z * _sigmoid(z))).astype(BF16)
    out = (jnp.dot(yc, wout_ref[0:w_conv, :], preferred_element_type=F32)
           + jnp.dot(ym_ref[...], wout_ref[w_conv:, :], preferred_element_type=F32))
    ms = jnp.mean(out * out, axis=-1, keepdims=True)
    o_ref[...] = x_ref[...] + (out * lax.rsqrt(ms + EPS)) * (gt_ref[0] * gpost_ref[...])


def _out_proj(cva, cvb, cvb_col, p, ym, x2d, ada3, row_of_tile, ln_g, ln_b, g_post, w_pw2, w_out,
              *, tm, w_conv):
    m, d = x2d.shape
    half = w_conv // 2
    w_mix = w_out.shape[0]
    once = pl.Buffered(1)
    return pl.pallas_call(
        _out_kernel,
        out_shape=jax.ShapeDtypeStruct((m, d), F32),
        grid=(m // tm,),
        in_specs=[pl.BlockSpec((tm, half), lambda i: (i, 0)),
                  pl.BlockSpec((tm, half), lambda i: (i, cvb_col)),
                  pl.BlockSpec((tm, w_conv), lambda i: (i, 2)),
                  pl.BlockSpec((tm, w_mix - w_conv), lambda i: (i, 0)),
                  pl.BlockSpec((tm, d), lambda i: (i, 0)),
                  pl.BlockSpec((1, 1, d), lambda i: (row_of_tile(i) * 3 + 2, 0, 0)),
                  pl.BlockSpec((1, w_conv), lambda i: (0, 0)),
                  pl.BlockSpec((1, w_conv), lambda i: (0, 0)),
                  pl.BlockSpec((1, d), lambda i: (0, 0)),
                  pl.BlockSpec((w_conv, w_conv), lambda i: (0, 0), pipeline_mode=once),
                  pl.BlockSpec((w_mix, d), lambda i: (0, 0), pipeline_mode=once)],
        out_specs=pl.BlockSpec((tm, d), lambda i: (i, 0)),
        compiler_params=_cparams(("arbitrary",)),
        name="out_proj",
    )(cva, cvb, p, ym, x2d, ada3, ln_g, ln_b, g_post, w_pw2, w_out)


def kernel(x, c, ctx, c_ctx, w_ada, b_ada, g_pre, g_post, w_in, b_gate, w_dw, b_dw, ln_g, ln_b, w_pw2,
           g_head, w_out):
    bsz, seq, d = x.shape
    ctx_len = ctx.shape[1]
    depth = w_ada.shape[0]
    w_conv = w_dw.shape[-1]
    w_ml = g_head.shape[-1]
    nh = b_gate.shape[-1] // 4
    n_main = 3 * w_conv + 5 * w_ml
    half = w_conv // 2
    assert w_ml == nh * LANES and 4 * nh <= LANES and w_conv == w_ml
    assert seq % CHUNK == 0 and ctx_len % CHUNK == 0 and seq % GRID_W == 0 and half % LANES == 0

    ctx_row = bsz
    cond = jnp.concatenate([c, c_ctx[None, :], jnp.zeros((8 - bsz - 1, d), F32)], axis=0)
    ada = _ada_all_layers(cond, w_ada, b_ada)
    ada3 = ada.reshape(depth * 8 * 3, 1, d)

    w_in_t = jnp.swapaxes(w_in, 1, 2)
    bias_g = jnp.pad(b_gate, ((0, 0), (0, LANES - 4 * nh)))

    tm_x = min(1024, seq)
    tm_c = min(1024, bsz * ctx_len)
    tn = min(1024, w_conv)
    tm_o = min(512, seq)
    tm_oc = min(256, ctx_len)
    cw = min(256, half)

    xs = x.reshape(bsz * seq, d)
    cs = ctx.reshape(bsz * ctx_len, d)
    for l in range(depth):
        last = l == depth - 1
        row_x = lambda i, l=l, t=seq // tm_x: l * 8 + i // t
        row_c = lambda i, l=l: l * 8 + ctx_row + 0 * i
        pc, gc, w_main_t, w_pw2_b, w_out_b = _in_proj(
            cs, ada3, row_c, g_pre[l][None], None, w_in_t, tm=tm_c, tn=tn // 2, w_conv=w_conv,
            w_ml=w_ml, nh=nh, layer=l, cast_also=(w_pw2, w_out))
        px, gx = _in_proj(xs, ada3, row_x, g_pre[l][None], w_main_t, w_in_t,
                          tm=tm_x, tn=2 * tn, w_conv=w_conv, w_ml=w_ml, nh=nh, layer=l)
        gates_x = _gate_prep(gx, bias_g[l][None], nh, min(1024, seq))
        gates_c = _gate_prep(gc, bias_g[l][None], nh, min(1024, ctx_len))
        ymx, ymc = _mlstm(px, pc, gates_x, gates_c, g_head[l][None], bsz=bsz, seq=seq,
                          ctx_len=ctx_len, nh=nh, w_conv=w_conv, w_ml=w_ml)
        ncol = half // cw
        g0 = w_conv // cw
        cv_w = _segconv(px, w_dw[l], b_dw[l][None], seg=GRID_W, rows=min(512, seq), cw=cw, ncol=ncol,
                        a_col0=0, g_col0=g0, w_col0=0)
        cv_h = _rowconv(px, w_dw[l], b_dw[l][None], seq=seq, cw=cw, ncol=ncol,
                        a_col0=ncol, g_col0=g0 + ncol, w_col0=ncol)
        row_xo = lambda i, l=l, t=seq // tm_o: l * 8 + i // t
        xs_new = _out_proj(cv_w, cv_h, 0, px, ymx, xs, ada3, row_xo, ln_g[l][None], ln_b[l][None],
                           g_post[l][None], w_pw2_b, w_out_b, tm=tm_o, w_conv=w_conv)
        if not last:
            cv_c = _segconv(pc, w_dw[l], b_dw[l][None], seg=ctx_len, rows=ctx_len, cw=cw,
                            ncol=2 * ncol, a_col0=0, g_col0=g0, w_col0=0)
            cs = _out_proj(cv_c, cv_c, 1, pc, ymc, cs, ada3, row_c, ln_g[l][None], ln_b[l][None],
                           g_post[l][None], w_pw2_b, w_out_b, tm=tm_oc, w_conv=w_conv)
        xs = xs_new
    return xs.reshape(bsz, seq, d)
```

```python
import functools

import jax
import jax.numpy as jnp
from jax import lax
from jax.experimental import pallas as pl
from jax.experimental.pallas import tpu as pltpu

F32 = jnp.float32
BF16 = jnp.bfloat16

GRID_W = 64
CONV_K = 31
CONV_PAD = CONV_K // 2
EPS = 1e-6
NEG = -1e30
LOG2E = 1.4426950408889634
BF16_MAX = (2.0 - 2.0 ** -7) * 2.0 ** 127
LANES = 128
SUBLANES = 8
CHUNK = 128
SEG_PAD = 16
VMEM_LIMIT = 56 * 1024 * 1024


def _cparams(sem):
    return pltpu.CompilerParams(dimension_semantics=sem, vmem_limit_bytes=VMEM_LIMIT)


def _sigmoid(x):
    return 1.0 / (1.0 + jnp.exp(-x))


def _ada_kernel(c_ref, w_ref, b_ref, o_ref):
    c = c_ref[...]
    s = (c * _sigmoid(c)).astype(BF16)
    o_ref[0] = jnp.dot(s, w_ref[0].astype(BF16), preferred_element_type=F32) + b_ref[0]


def _ada_all_layers(cond, w_ada, b_ada):
    depth, d, n = w_ada.shape
    tn = 1024 if n % 1024 == 0 else n // 3
    return pl.pallas_call(
        _ada_kernel,
        out_shape=jax.ShapeDtypeStruct((depth, 8, n), F32),
        grid=(depth, n // tn),
        in_specs=[pl.BlockSpec((8, d), lambda l, j: (0, 0)),
                  pl.BlockSpec((1, d, tn), lambda l, j: (l, 0, j)),
                  pl.BlockSpec((1, 1, tn), lambda l, j: (l, 0, j))],
        out_specs=pl.BlockSpec((1, 8, tn), lambda l, j: (l, 0, j)),
        compiler_params=_cparams(("arbitrary", "arbitrary")),
        name="ada",
    )(cond, w_ada, b_ada.reshape(depth, 1, n))


_NT = (((1,), (1,)), ((), ()))


def _in_kernel(x_ref, sh_ref, sc_ref, gpre_ref, w_ref, wg_ref, *rest,
               tn, q_lo, q_hi, q_scale, sub, emit_w):
    if emit_w:
        wpw32_ref, wout32_ref, p_ref, g_ref, wb_ref, wpwb_ref, woutb_ref, hx_ref = rest
        wpwb_ref[...] = wpw32_ref[...].astype(BF16)
        woutb_ref[...] = wout32_ref[...].astype(BF16)
    else:
        p_ref, g_ref, hx_ref = rest
    j = pl.program_id(1)

    @pl.when(j == 0)
    def _():
        tm, d = x_ref.shape
        gain = gpre_ref[...] * (1.0 + sc_ref[0])
        for r in range(0, tm, sub):
            x = x_ref[r:r + sub, :]
            ms = jnp.mean(x * x, axis=-1, keepdims=True)
            h = (x * lax.rsqrt(ms + EPS)) * gain + sh_ref[0]
            hx_ref[r:r + sub, :] = h.astype(BF16)
        wg = wg_ref[...].astype(BF16)
        wg = jnp.concatenate([wg, jnp.zeros((LANES - wg.shape[0], d), BF16)], axis=0)
        g_ref[...] = lax.dot_general(hx_ref[...], wg, _NT, preferred_element_type=F32)

    w = w_ref[...]
    if emit_w:
        w = w.astype(BF16)
        wb_ref[...] = w
    acc = lax.dot_general(hx_ref[...], w, _NT, preferred_element_type=F32)
    col = j * tn + lax.broadcasted_iota(jnp.int32, (1, tn), 1)
    scale = jnp.where((col >= q_lo) & (col < q_hi), q_scale, 1.0).astype(F32)
    p_ref[...] = (acc * scale).astype(BF16)


def _in_proj(x2d, ada3, row_of_tile, g_pre, w_t, w_in_t, *, tm, tn, w_conv, w_ml, nh, layer,
             cast_also=None):
    m, d = x2d.shape
    n = 3 * w_conv + 5 * w_ml
    q_lo = 3 * w_conv
    steps = n // tn
    emit_w = cast_also is not None
    kern = functools.partial(_in_kernel, tn=tn, q_lo=q_lo, q_hi=q_lo + w_ml,
                             q_scale=float(LANES) ** -0.5, sub=min(tm, 256), emit_w=emit_w)
    in_specs = [pl.BlockSpec((tm, d), lambda i, j: (i, 0)),
                pl.BlockSpec((1, 1, d), lambda i, j: (row_of_tile(i) * 3 + 0, 0, 0)),
                pl.BlockSpec((1, 1, d), lambda i, j: (row_of_tile(i) * 3 + 1, 0, 0)),
                pl.BlockSpec((1, d), lambda i, j: (0, 0)),
                pl.BlockSpec((tn, d), lambda i, j: (j, 0)),
                pl.BlockSpec((None, 4 * nh, d), lambda i, j: (layer, n // (4 * nh), 0))]
    args = [x2d, ada3, ada3, g_pre, w_t, w_in_t]
    out_shape = [jax.ShapeDtypeStruct((m, n), BF16), jax.ShapeDtypeStruct((m, LANES), F32)]
    out_specs = [pl.BlockSpec((tm, tn), lambda i, j: (i, j)),
                 pl.BlockSpec((tm, LANES), lambda i, j: (i, 0))]
    if emit_w:
        assert m == tm
        in_specs[4] = pl.BlockSpec((None, tn, d), lambda i, j: (layer, j, 0))
        args[4] = w_in_t
        out_shape.append(jax.ShapeDtypeStruct((n, d), BF16))
        out_specs.append(pl.BlockSpec((tn, d), lambda i, j: (j, 0)))
        for w32 in cast_also:
            rows, cols = w32.shape[1] // steps, w32.shape[2]
            in_specs.append(pl.BlockSpec((None, rows, cols), lambda i, j: (layer, j, 0)))
            args.append(w32)
            out_shape.append(jax.ShapeDtypeStruct(w32.shape[1:], BF16))
            out_specs.append(pl.BlockSpec((rows, cols), lambda i, j: (j, 0)))
    return pl.pallas_call(
        kern,
        out_shape=tuple(out_shape),
        grid=(m // tm, steps),
        in_specs=in_specs,
        out_specs=tuple(out_specs),
        scratch_shapes=[pltpu.VMEM((tm, d), BF16)],
        compiler_params=_cparams(("arbitrary", "arbitrary")),
        name="in_proj",
    )(*args)


def _gate_kernel(g_ref, bias_ref, bt_ref, rt_ref, ct_ref, *, nh):
    row = lax.broadcasted_iota(jnp.int32, (CHUNK, CHUNK), 0)
    col = lax.broadcasted_iota(jnp.int32, (CHUNK, CHUNK), 1)
    tril = jnp.where(row >= col, 1.0, 0.0).astype(BF16)
    is_ff = (col >= nh) & (col < 2 * nh)
    is_fb = (col >= 3 * nh) & (col < 4 * nh)
    grow = lax.broadcasted_iota(jnp.int32, (4 * nh, CHUNK), 0)
    tok = lax.broadcasted_iota(jnp.int32, (4 * nh, CHUNK), 1)

    def csum(tri, parts):
        out = jnp.dot(tri, parts[0], preferred_element_type=F32)
        for p in parts[1:]:
            out = out + jnp.dot(tri, p, preferred_element_type=F32)
        return out

    for c in range(g_ref.shape[0] // CHUNK):
        g = g_ref[c * CHUNK:(c + 1) * CHUNK, :] + bias_ref[...]
        lf = jnp.minimum(g, 0.0) - jnp.log(1.0 + jnp.exp(-jnp.abs(g)))
        hi = lf.astype(BF16)
        r1 = lf - hi.astype(F32)
        mid = r1.astype(BF16)
        lo = (r1 - mid.astype(F32)).astype(BF16)
        pre = csum(tril, (hi, mid, lo))
        suf = pre[CHUNK - 1:CHUNK, :] - pre + lf
        bmat = jnp.where(is_ff, pre, jnp.where(is_fb, suf, 0.0))
        li = pltpu.roll(g, nh, axis=1)
        rmat = li - bmat
        bt_ref[c] = bmat.T[0:4 * nh, :] * LOG2E
        rt = rmat.T[0:4 * nh, :] * LOG2E
        rt_ref[c] = rt
        cf = rt
        cb = rt
        sh = 1
        while sh < CHUNK:
            cf = jnp.maximum(cf, jnp.where(tok >= sh, pltpu.roll(cf, sh, axis=1), NEG))
            cb = jnp.maximum(cb, jnp.where(tok < CHUNK - sh, pltpu.roll(cb, CHUNK - sh, axis=1), NEG))
            sh *= 2
        ct_ref[c] = jnp.where(grow < 2 * nh, cf, cb)


def _gate_prep(g2d, bias, nh, rows):
    m = g2d.shape[0]
    nc = rows // CHUNK
    out = jax.ShapeDtypeStruct((m // CHUNK, 4 * nh, CHUNK), F32)
    ospec = pl.BlockSpec((nc, 4 * nh, CHUNK), lambda i: (i, 0, 0))
    return pl.pallas_call(
        functools.partial(_gate_kernel, nh=nh),
        out_shape=(out, out, out),
        grid=(m // rows,),
        in_specs=[pl.BlockSpec((rows, LANES), lambda i: (i, 0)),
                  pl.BlockSpec((1, LANES), lambda i: (0, 0))],
        out_specs=(ospec, ospec, ospec),
        compiler_params=_cparams(("arbitrary",)),
        name="gate_prep",
    )(g2d, bias)


def _mlstm_kernel(qx_ref, kx_ref, vx_ref, ox_ref, zx_ref, qc_ref, kc_ref, vc_ref, oc_ref, zc_ref,
                  btx_ref, rtx_ref, ctx_ref, btc_ref, rtc_ref, ctc_ref, gh_ref, yx_ref, yc_ref,
                  s_s, kv_s, sc_s, cp_s, mp_s, *, nh, group):
    h_idx = pl.program_id(1)
    row = lax.broadcasted_iota(jnp.int32, (CHUNK, CHUNK), 0)
    col = lax.broadcasted_iota(jnp.int32, (CHUNK, CHUNK), 1)
    masks = (row >= col, row <= col)
    ones_blk = jnp.ones((CHUNK, LANES), BF16)
    eye = row == col
    ncx = qx_ref.shape[0] // CHUNK
    ncc = qc_ref.shape[0] // CHUNK
    nct = ncc + ncx
    g_rows = (nh + h_idx, 3 * nh + h_idx)
    last = (CHUNK - 1, 0)

    trips = ncx // group
    ctx_per_trip = ncc // trips

    def both_segments(chunk_fn, latent_refs, context_refs):
        def body(i, _):
            for u in range(group):
                chunk_fn(i * group + u, ncc, *latent_refs)
            for u in range(ctx_per_trip):
                chunk_fn(i * ctx_per_trip + u, 0, *context_refs)
            return 0
        lax.fori_loop(0, trips, body, 0)

    def chunk_a(c, c0, q_ref, k_ref, v_ref, bt_ref, rt_ref, ct_ref):
        r0 = pl.multiple_of(c * CHUNK, CHUNK)
        rs = pl.multiple_of((c0 + c) * CHUNK, CHUNK)
        kt = k_ref[pl.ds(r0, CHUNK), :].astype(F32).T
        s_s[pl.ds(rs, CHUNK), :] = jnp.dot(q_ref[pl.ds(r0, CHUNK), :], kt.astype(BF16),
                                           preferred_element_type=F32)
        vaug = jnp.concatenate([v_ref[pl.ds(r0, CHUNK), :], ones_blk], axis=1)
        for d in range(2):
            b_last = bt_ref[c, pl.ds(g_rows[d], 1), last[d]:last[d] + 1]
            mg = b_last + ct_ref[c, pl.ds(g_rows[d], 1), last[d]:last[d] + 1]
            gl = b_last + rt_ref[c, pl.ds(g_rows[d], 1), :]
            kws = (kt * jnp.exp2(gl - mg)).astype(BF16)
            kv_s[d, c0 + c] = jnp.dot(kws, vaug, preferred_element_type=F32)
            sc_s[d, c0 + c, 0:1, :] = jnp.broadcast_to(b_last, (1, LANES))
            sc_s[d, c0 + c, 1:2, :] = jnp.broadcast_to(mg, (1, LANES))

    both_segments(chunk_a, (qx_ref, kx_ref, vx_ref, btx_ref, rtx_ref, ctx_ref),
                  (qc_ref, kc_ref, vc_ref, btc_ref, rtc_ref, ctc_ref))

    def scan(d):
        def body(j, carry):
            m_prev, c_prev = carry
            g = j if d == 0 else jnp.where(j < ncc, ncc - 1 - j, nct - 1 - j + ncc)
            cp_s[d, g] = c_prev.astype(BF16)
            mp_s[d, g] = jnp.broadcast_to(m_prev, (8, LANES))
            b_last = sc_s[d, g, 0:1, :]
            mg = sc_s[d, g, 1:2, :]
            m_new = jnp.maximum(b_last + m_prev, mg)
            a = jnp.exp2(b_last + m_prev - m_new)
            w = jnp.exp2(mg - m_new)
            c_new = (jnp.concatenate([a, a], axis=1) * c_prev
                     + jnp.concatenate([w, w], axis=1) * kv_s[d, g])
            return (m_new, c_new)
        lax.fori_loop(0, nct, body, (jnp.full((1, LANES), NEG, F32),
                                     jnp.zeros((LANES, 2 * LANES), F32)), unroll=2)

    scan(0)
    scan(1)

    def chunk_b(c, c0, q_ref, v_ref, o_ref, z_ref, bt_ref, rt_ref, ct_ref, y_ref):
        r0 = pl.multiple_of(c * CHUNK, CHUNK)
        rs = pl.multiple_of((c0 + c) * CHUNK, CHUNK)
        q = q_ref[pl.ds(r0, CHUNK), :].astype(F32)
        s = s_s[pl.ds(rs, CHUNK), :]
        vaug = jnp.concatenate([v_ref[pl.ds(r0, CHUNK), :], ones_blk], axis=1)
        h = None
        for d in range(2):
            c_row = ct_ref[c, pl.ds(g_rows[d], 1), :]
            r_row = rt_ref[c, pl.ds(g_rows[d], 1), :]
            m_prev = mp_s[d, c0 + c, 0:1, :]
            floor_row = jnp.minimum(
                jnp.exp2(-(bt_ref[c, pl.ds(g_rows[d], 1), :] + jnp.maximum(c_row, m_prev))), BF16_MAX)
            floor_bc = jnp.dot(jnp.where(eye, floor_row, 0.0).astype(BF16), ones_blk,
                               preferred_element_type=F32)
            cm_bc = jnp.broadcast_to(c_row, (CHUNK, CHUNK)).T
            mm = jnp.maximum(cm_bc, m_prev)
            qw = (q * jnp.exp2(m_prev - mm)).astype(BF16)
            pm = (jnp.exp2(jnp.where(masks[d], r_row - mm, NEG)) * s).astype(BF16)
            nd = jnp.dot(jnp.concatenate([qw, pm], axis=1),
                         jnp.concatenate([cp_s[d, c0 + c], vaug], axis=0),
                         preferred_element_type=F32)
            hd = nd[:, 0:LANES] / jnp.maximum(jnp.abs(nd[:, LANES:]), floor_bc)
            h = hd if h is None else h + hd
        mu = jnp.mean(h, axis=1, keepdims=True)
        hc = h - mu
        hn = hc * lax.rsqrt(jnp.mean(hc * hc, axis=1, keepdims=True) + EPS) * gh_ref[...]
        z = z_ref[pl.ds(r0, CHUNK), :].astype(F32)
        o = o_ref[pl.ds(r0, CHUNK), :].astype(F32)
        y = hn * (z / ((1.0 + jnp.exp(-o)) * (1.0 + jnp.exp(-z))))
        y_ref[pl.ds(r0, CHUNK), :] = y.astype(BF16)

    both_segments(chunk_b, (qx_ref, vx_ref, ox_ref, zx_ref, btx_ref, rtx_ref, ctx_ref, yx_ref),
                  (qc_ref, vc_ref, oc_ref, zc_ref, btc_ref, rtc_ref, ctc_ref, yc_ref))


def _mlstm(px, pc, gates_x, gates_c, g_head, *, bsz, seq, ctx_len, nh, w_conv, w_ml):
    dh = LANES
    cb = 3 * w_conv // dh
    hb = w_ml // dh

    def colspec(rows, k):
        return pl.BlockSpec((rows, dh), lambda b, h: (b, cb + k * hb + h))

    def gspec(rows):
        return pl.BlockSpec((rows // CHUNK, 4 * nh, CHUNK), lambda b, h: (b, 0, 0))

    tot = seq + ctx_len
    nct = tot // CHUNK
    group = min(8, seq // CHUNK)
    assert (seq // CHUNK) % group == 0 and (ctx_len // CHUNK) % (seq // CHUNK // group) == 0
    return pl.pallas_call(
        functools.partial(_mlstm_kernel, nh=nh, group=group),
        out_shape=(jax.ShapeDtypeStruct((bsz * seq, w_ml), BF16),
                   jax.ShapeDtypeStruct((bsz * ctx_len, w_ml), BF16)),
        grid=(bsz, nh),
        in_specs=[colspec(seq, 0), colspec(seq, 1), colspec(seq, 2), colspec(seq, 3), colspec(seq, 4),
                  colspec(ctx_len, 0), colspec(ctx_len, 1), colspec(ctx_len, 2), colspec(ctx_len, 3),
                  colspec(ctx_len, 4),
                  gspec(seq), gspec(seq), gspec(seq), gspec(ctx_len), gspec(ctx_len), gspec(ctx_len),
                  pl.BlockSpec((1, dh), lambda b, h: (0, h))],
        out_specs=(pl.BlockSpec((seq, dh), lambda b, h: (b, h)),
                   pl.BlockSpec((ctx_len, dh), lambda b, h: (b, h))),
        scratch_shapes=[pltpu.VMEM((tot, CHUNK), F32),
                        pltpu.VMEM((2, nct, dh, 2 * dh), F32),
                        pltpu.VMEM((2, nct, 8, LANES), F32),
                        pltpu.VMEM((2, nct, dh, 2 * dh), BF16),
                        pltpu.VMEM((2, nct, 8, LANES), F32)],
        compiler_params=_cparams(("arbitrary", "arbitrary")),
        name="mlstm",
    )(px, px, px, px, px, pc, pc, pc, pc, pc, *gates_x, *gates_c, g_head)


def _glu(a_ref, g_ref, r0, rows):
    a = a_ref[r0:r0 + rows, :].astype(F32)
    g = g_ref[r0:r0 + rows, :].astype(F32)
    return a * (0.5 * jnp.tanh(0.5 * g) + 0.5)


def _segconv_kernel(a_ref, g_ref, w_ref, b_ref, o_ref, pad_ref, *, seg, sub):
    rows, cw = a_ref.shape
    stride = seg + 2 * SEG_PAD
    total = rows // seg * stride
    zeros = jnp.zeros((SEG_PAD, cw), F32)
    for s in range(rows // seg):
        pad_ref[0, s * stride:s * stride + SEG_PAD, :] = zeros
        pad_ref[0, s * stride + SEG_PAD:s * stride + SEG_PAD + seg, :] = _glu(a_ref, g_ref, s * seg, seg)
        pad_ref[0, s * stride + SEG_PAD + seg:(s + 1) * stride, :] = zeros
    pad_ref[0, total:total + SUBLANES, :] = jnp.zeros((SUBLANES, cw), F32)
    for r in range(1, SUBLANES):
        pad_ref[r, 0:total, :] = pad_ref[0, r:r + total, :]
    for s in range(rows // seg):
        for t0 in range(0, seg, sub):
            start = s * stride + SEG_PAD + t0 - CONV_PAD
            acc = jnp.zeros((sub, cw), F32) + b_ref[...]
            for k in range(CONV_K):
                r = (start + k) % SUBLANES
                acc = acc + pad_ref[r, start + k - r:start + k - r + sub, :] * w_ref[k:k + 1, :]
            o_ref[s * seg + t0:s * seg + t0 + sub, :] = acc


def _segconv(p, w_dw, b_dw, *, seg, rows, cw, ncol, a_col0, g_col0, w_col0):
    m = p.shape[0]
    stride = seg + 2 * SEG_PAD
    return pl.pallas_call(
        functools.partial(_segconv_kernel, seg=seg, sub=min(seg, 64)),
        out_shape=jax.ShapeDtypeStruct((m, ncol * cw), F32),
        grid=(m // rows, ncol),
        in_specs=[pl.BlockSpec((rows, cw), lambda i, j: (i, a_col0 + j)),
                  pl.BlockSpec((rows, cw), lambda i, j: (i, g_col0 + j)),
                  pl.BlockSpec((CONV_K, cw), lambda i, j: (0, w_col0 + j)),
                  pl.BlockSpec((1, cw), lambda i, j: (0, w_col0 + j))],
        out_specs=pl.BlockSpec((rows, cw), lambda i, j: (i, j)),
        scratch_shapes=[pltpu.VMEM((SUBLANES, rows // seg * stride + SUBLANES, cw), F32)],
        compiler_params=_cparams(("arbitrary", "arbitrary")),
        name="segconv",
    )(p, p, w_dw, b_dw)


def _rowconv_kernel(a_ref, g_ref, w_ref, b_ref, o_ref, u_ref, *, nrows):
    cw = a_ref.shape[1]
    for r in range(nrows):
        u_ref[r * GRID_W:(r + 1) * GRID_W, :] = _glu(a_ref, g_ref, r * GRID_W, GRID_W)
    for r in range(nrows):
        acc = jnp.zeros((GRID_W, cw), F32) + b_ref[...]
        for k in range(CONV_K):
            src = r + k - CONV_PAD
            if 0 <= src < nrows:
                acc = acc + u_ref[src * GRID_W:(src + 1) * GRID_W, :] * w_ref[k:k + 1, :]
        o_ref[r * GRID_W:(r + 1) * GRID_W, :] = acc


def _rowconv(p, w_dw, b_dw, *, seq, cw, ncol, a_col0, g_col0, w_col0):
    m = p.shape[0]
    nrows = seq // GRID_W
    return pl.pallas_call(
        functools.partial(_rowconv_kernel, nrows=nrows),
        out_shape=jax.ShapeDtypeStruct((m, ncol * cw), F32),
        grid=(m // seq, ncol),
        in_specs=[pl.BlockSpec((seq, cw), lambda i, j: (i, a_col0 + j)),
                  pl.BlockSpec((seq, cw), lambda i, j: (i, g_col0 + j)),
                  pl.BlockSpec((CONV_K, cw), lambda i, j: (0, w_col0 + j)),
                  pl.BlockSpec((1, cw), lambda i, j: (0, w_col0 + j))],
        out_specs=pl.BlockSpec((seq, cw), lambda i, j: (i, j)),
        scratch_shapes=[pltpu.VMEM((seq, cw), F32)],
        compiler_params=_cparams(("arbitrary", "arbitrary")),
        name="rowconv",
    )(p, p, w_dw, b_dw)


def _out_kernel(cva_ref, cvb_ref, z_ref, ym_ref, x_ref, gt_ref, lng_ref, lnb_ref, gpost_ref,
                wpw_ref, wout_ref, o_ref):
    w_conv = wpw_ref.shape[0]
    u = jnp.concatenate([cva_ref[...], cvb_ref[...]], axis=1)
    mu = jnp.mean(u, axis=-1, keepdims=True)
    uc = u - mu
    r = uc * lax.rsqrt(jnp.mean(uc * uc, axis=-1, keepdims=True) + EPS) * lng_ref[...] + lnb_ref[...]
    t = jnp.dot((r * _sigmoid(r)).astype(BF16), wpw_ref[...], preferred_element_type=F32)
    z = z_ref[...].astype(F32)
    yc = (t * (z * _sigmoid(z))).astype(BF16)
    out = (jnp.dot(yc, wout_ref[0:w_conv, :], preferred_element_type=F32)
           + jnp.dot(ym_ref[...], wout_ref[w_conv:, :], preferred_element_type=F32))
    ms = jnp.mean(out * out, axis=-1, keepdims=True)
    o_ref[...] = x_ref[...] + (out * lax.rsqrt(ms + EPS)) * (gt_ref[0] * gpost_ref[...])


def _out_proj(cva, cvb, cvb_col, p, ym, x2d, ada3, row_of_tile, ln_g, ln_b, g_post, w_pw2, w_out,
              *, tm, w_conv):
    m, d = x2d.shape
    half = w_conv // 2
    w_mix = w_out.shape[0]
    once = pl.Buffered(1)
    return pl.pallas_call(
        _out_kernel,
        out_shape=jax.ShapeDtypeStruct((m, d), F32),
        grid=(m // tm,),
        in_specs=[pl.BlockSpec((tm, half), lambda i: (i, 0)),
                  pl.BlockSpec((tm, half), lambda i: (i, cvb_col)),
                  pl.BlockSpec((tm, w_conv), lambda i: (i, 2)),
                  pl.BlockSpec((tm, w_mix - w_conv), lambda i: (i, 0)),
                  pl.BlockSpec((tm, d), lambda i: (i, 0)),
                  pl.BlockSpec((1, 1, d), lambda i: (row_of_tile(i) * 3 + 2, 0, 0)),
                  pl.BlockSpec((1, w_conv), lambda i: (0, 0)),
                  pl.BlockSpec((1, w_conv), lambda i: (0, 0)),
                  pl.BlockSpec((1, d), lambda i: (0, 0)),
                  pl.BlockSpec((w_conv, w_conv), lambda i: (0, 0), pipeline_mode=once),
                  pl.BlockSpec((w_mix, d), lambda i: (0, 0), pipeline_mode=once)],
        out_specs=pl.BlockSpec((tm, d), lambda i: (i, 0)),
        compiler_params=_cparams(("arbitrary",)),
        name="out_proj",
    )(cva, cvb, p, ym, x2d, ada3, ln_g, ln_b, g_post, w_pw2, w_out)


def kernel(x, c, ctx, c_ctx, w_ada, b_ada, g_pre, g_post, w_in, b_gate, w_dw, b_dw, ln_g, ln_b, w_pw2,
           g_head, w_out):
    bsz, seq, d = x.shape
    ctx_len = ctx.shape[1]
    depth = w_ada.shape[0]
    w_conv = w_dw.shape[-1]
    w_ml = g_head.shape[-1]
    nh = b_gate.shape[-1] // 4
    n_main = 3 * w_conv + 5 * w_ml
    half = w_conv // 2
    assert w_ml == nh * LANES and 4 * nh <= LANES and w_conv == w_ml
    assert seq % CHUNK == 0 and ctx_len % CHUNK == 0 and seq % GRID_W == 0 and half % LANES == 0

    ctx_row = bsz
    cond = jnp.concatenate([c, c_ctx[None, :], jnp.zeros((8 - bsz - 1, d), F32)], axis=0)
    ada = _ada_all_layers(cond, w_ada, b_ada)
    ada3 = ada.reshape(depth * 8 * 3, 1, d)

    w_in_t = jnp.swapaxes(w_in, 1, 2)
    bias_g = jnp.pad(b_gate, ((0, 0), (0, LANES - 4 * nh)))

    tm_x = min(1024, seq)
    tm_c = min(1024, bsz * ctx_len)
    tn = min(1024, w_conv)
    tm_o = min(512, seq)
    tm_oc = min(256, ctx_len)
    cw = min(256, half)

    xs = x.reshape(bsz * seq, d)
    cs = ctx.reshape(bsz * ctx_len, d)
    for l in range(depth):
        last = l == depth - 1
        row_x = lambda i, l=l, t=seq // tm_x: l * 8 + i // t
        row_c = lambda i, l=l: l * 8 + ctx_row + 0 * i
        pc, gc, w_main_t, w_pw2_b, w_out_b = _in_proj(
            cs, ada3, row_c, g_pre[l][None], None, w_in_t, tm=tm_c, tn=tn // 2, w_conv=w_conv,
            w_ml=w_ml, nh=nh, layer=l, cast_also=(w_pw2, w_out))
        px, gx = _in_proj(xs, ada3, row_x, g_pre[l][None], w_main_t, w_in_t,
                          tm=tm_x, tn=2 * tn, w_conv=w_conv, w_ml=w_ml, nh=nh, layer=l)
        gates_x = _gate_prep(gx, bias_g[l][None], nh, min(1024, seq))
        gates_c = _gate_prep(gc, bias_g[l][None], nh, min(1024, ctx_len))
        ymx, ymc = _mlstm(px, pc, gates_x, gates_c, g_head[l][None], bsz=bsz, seq=seq,
                          ctx_len=ctx_len, nh=nh, w_conv=w_conv, w_ml=w_ml)
        ncol = half // cw
        g0 = w_conv // cw
        cv_w = _segconv(px, w_dw[l], b_dw[l][None], seg=GRID_W, rows=min(512, seq), cw=cw, ncol=ncol,
                        a_col0=0, g_col0=g0, w_col0=0)
        cv_h = _rowconv(px, w_dw[l], b_dw[l][None], seq=seq, cw=cw, ncol=ncol,
                        a_col0=ncol, g_col0=g0 + ncol, w_col0=ncol)
        row_xo = lambda i, l=l, t=seq // tm_o: l * 8 + i // t
        xs_new = _out_proj(cv_w, cv_h, 0, px, ymx, xs, ada3, row_xo, ln_g[l][None], ln_b[l][None],
                           g_post[l][None], w_pw2_b, w_out_b, tm=tm_o, w_conv=w_conv)
        if not last:
            cv_c = _segconv(pc, w_dw[l], b_dw[l][None], seg=ctx_len, rows=ctx_len, cw=cw,
                            ncol=2 * ncol, a_col0=0, g_col0=g0, w_col0=0)
            cs = _out_proj(cv_c, cv_c, 1, pc, ymc, cs, ada3, row_c, ln_g[l][None], ln_b[l][None],
                           g_post[l][None], w_pw2_b, w_out_b, tm=tm_oc, w_conv=w_conv)
        xs = xs_new
    return xs.reshape(bsz, seq, d)
```

```python
import functools

import jax
import jax.numpy as jnp
from jax import lax
from jax.experimental import pallas as pl
from jax.experimental.pallas import tpu as pltpu

F32 = jnp.float32
BF16 = jnp.bfloat16

GRID_W = 64
CONV_K = 31
CONV_PAD = CONV_K // 2
EPS = 1e-6
NEG = -1e30
LOG2E = 1.4426950408889634
BF16_MAX = (2.0 - 2.0 ** -7) * 2.0 ** 127
LANES = 128
SUBLANES = 8
CHUNK = 128
SEG_PAD = 16
VMEM_LIMIT = 56 * 1024 * 1024


def _cparams(sem):
    return pltpu.CompilerParams(dimension_semantics=sem, vmem_limit_bytes=VMEM_LIMIT)


def _sigmoid(x):
    return 1.0 / (1.0 + jnp.exp(-x))


def _ada_tile(c_ref, w_ref, b_ref):
    c = c_ref[...]
    s = (c * _sigmoid(c)).astype(BF16)
    return jnp.dot(s, w_ref[...].astype(BF16), preferred_element_type=F32) + b_ref[...]


def _ada_kernel(c_ref, w_ref, b_ref, o_ref):
    o_ref[0] = _ada_tile(c_ref, w_ref.at[0], b_ref.at[0])


def _ada_layers(cond, w_ada, b_ada, layers):
    depth, d, n = w_ada.shape
    tn = 1024 if n % 1024 == 0 else n // 3
    return pl.pallas_call(
        _ada_kernel,
        out_shape=jax.ShapeDtypeStruct((layers, 8, n), F32),
        grid=(layers, n // tn),
        in_specs=[pl.BlockSpec((8, d), lambda l, j: (0, 0)),
                  pl.BlockSpec((1, d, tn), lambda l, j: (l, 0, j)),
                  pl.BlockSpec((1, 1, tn), lambda l, j: (l, 0, j))],
        out_specs=pl.BlockSpec((1, 8, tn), lambda l, j: (l, 0, j)),
        compiler_params=_cparams(("arbitrary", "arbitrary")),
        name="ada",
    )(cond, w_ada, b_ada.reshape(depth, 1, n))


_NT = (((1,), (1,)), ((), ()))


def _in_kernel(x_ref, sh_ref, sc_ref, gpre_ref, w_ref, wg_ref, *rest,
               tn, q_lo, q_hi, q_scale, sub, emit_w):
    if emit_w:
        wpw32_ref, wout32_ref, p_ref, g_ref, wb_ref, wpwb_ref, woutb_ref, hx_ref = rest
        wpwb_ref[...] = wpw32_ref[...].astype(BF16)
        woutb_ref[...] = wout32_ref[...].astype(BF16)
    else:
        p_ref, g_ref, hx_ref = rest
    j = pl.program_id(1)

    @pl.when(j == 0)
    def _():
        tm, d = x_ref.shape
        gain = gpre_ref[...] * (1.0 + sc_ref[0])
        for r in range(0, tm, sub):
            x = x_ref[r:r + sub, :]
            ms = jnp.mean(x * x, axis=-1, keepdims=True)
            h = (x * lax.rsqrt(ms + EPS)) * gain + sh_ref[0]
            hx_ref[r:r + sub, :] = h.astype(BF16)
        wg = wg_ref[...].astype(BF16)
        wg = jnp.concatenate([wg, jnp.zeros((LANES - wg.shape[0], d), BF16)], axis=0)
        g_ref[...] = lax.dot_general(hx_ref[...], wg, _NT, preferred_element_type=F32)

    w = w_ref[...]
    if emit_w:
        w = w.astype(BF16)
        wb_ref[...] = w
    acc = lax.dot_general(hx_ref[...], w, _NT, preferred_element_type=F32)
    col = j * tn + lax.broadcasted_iota(jnp.int32, (1, tn), 1)
    scale = jnp.where((col >= q_lo) & (col < q_hi), q_scale, 1.0).astype(F32)
    p_ref[...] = (acc * scale).astype(BF16)


def _in_proj(x2d, ada3, row_of_tile, g_pre, w_t, w_in_t, *, tm, tn, w_conv, w_ml, nh, layer,
             cast_also=None):
    m, d = x2d.shape
    n = 3 * w_conv + 5 * w_ml
    q_lo = 3 * w_conv
    steps = n // tn
    emit_w = cast_also is not None
    kern = functools.partial(_in_kernel, tn=tn, q_lo=q_lo, q_hi=q_lo + w_ml,
                             q_scale=float(LANES) ** -0.5, sub=min(tm, 256), emit_w=emit_w)
    in_specs = [pl.BlockSpec((tm, d), lambda i, j: (i, 0)),
                pl.BlockSpec((1, 1, d), lambda i, j: (row_of_tile(i) * 3 + 0, 0, 0)),
                pl.BlockSpec((1, 1, d), lambda i, j: (row_of_tile(i) * 3 + 1, 0, 0)),
                pl.BlockSpec((1, d), lambda i, j: (0, 0)),
                pl.BlockSpec((tn, d), lambda i, j: (j, 0)),
                pl.BlockSpec((None, 4 * nh, d), lambda i, j: (layer, n // (4 * nh), 0))]
    args = [x2d, ada3, ada3, g_pre, w_t, w_in_t]
    out_shape = [jax.ShapeDtypeStruct((m, n), BF16), jax.ShapeDtypeStruct((m, LANES), F32)]
    out_specs = [pl.BlockSpec((tm, tn), lambda i, j: (i, j)),
                 pl.BlockSpec((tm, LANES), lambda i, j: (i, 0))]
    if emit_w:
        assert m == tm
        in_specs[4] = pl.BlockSpec((None, tn, d), lambda i, j: (layer, j, 0))
        args[4] = w_in_t
        out_shape.append(jax.ShapeDtypeStruct((n, d), BF16))
        out_specs.append(pl.BlockSpec((tn, d), lambda i, j: (j, 0)))
        for w32 in cast_also:
            rows, cols = w32.shape[1] // steps, w32.shape[2]
            in_specs.append(pl.BlockSpec((None, rows, cols), lambda i, j: (layer, j, 0)))
            args.append(w32)
            out_shape.append(jax.ShapeDtypeStruct(w32.shape[1:], BF16))
            out_specs.append(pl.BlockSpec((rows, cols), lambda i, j: (j, 0)))
    return pl.pallas_call(
        kern,
        out_shape=tuple(out_shape),
        grid=(m // tm, steps),
        in_specs=in_specs,
        out_specs=tuple(out_specs),
        scratch_shapes=[pltpu.VMEM((tm, d), BF16)],
        compiler_params=_cparams(("arbitrary", "arbitrary")),
        name="in_proj",
    )(*args)


def _gate_kernel(g_ref, bias_ref, bt_ref, rt_ref, ct_ref, *, nh):
    row = lax.broadcasted_iota(jnp.int32, (CHUNK, CHUNK), 0)
    col = lax.broadcasted_iota(jnp.int32, (CHUNK, CHUNK), 1)
    tril = jnp.where(row >= col, 1.0, 0.0).astype(BF16)
    is_ff = (col >= nh) & (col < 2 * nh)
    is_fb = (col >= 3 * nh) & (col < 4 * nh)
    grow = lax.broadcasted_iota(jnp.int32, (4 * nh, CHUNK), 0)
    tok = lax.broadcasted_iota(jnp.int32, (4 * nh, CHUNK), 1)

    def csum(tri, parts):
        out = jnp.dot(tri, parts[0], preferred_element_type=F32)
        for p in parts[1:]:
            out = out + jnp.dot(tri, p, preferred_element_type=F32)
        return out

    for c in range(g_ref.shape[0] // CHUNK):
        g = g_ref[c * CHUNK:(c + 1) * CHUNK, :] + bias_ref[...]
        lf = jnp.minimum(g, 0.0) - jnp.log(1.0 + jnp.exp(-jnp.abs(g)))
        hi = lf.astype(BF16)
        r1 = lf - hi.astype(F32)
        mid = r1.astype(BF16)
        lo = (r1 - mid.astype(F32)).astype(BF16)
        pre = csum(tril, (hi, mid, lo))
        suf = pre[CHUNK - 1:CHUNK, :] - pre + lf
        bmat = jnp.where(is_ff, pre, jnp.where(is_fb, suf, 0.0))
        li = pltpu.roll(g, nh, axis=1)
        rmat = li - bmat
        bt_ref[c] = bmat.T[0:4 * nh, :] * LOG2E
        rt = rmat.T[0:4 * nh, :] * LOG2E
        rt_ref[c] = rt
        cf = rt
        cb = rt
        sh = 1
        while sh < CHUNK:
            cf = jnp.maximum(cf, jnp.where(tok >= sh, pltpu.roll(cf, sh, axis=1), NEG))
            cb = jnp.maximum(cb, jnp.where(tok < CHUNK - sh, pltpu.roll(cb, CHUNK - sh, axis=1), NEG))
            sh *= 2
        ct_ref[c] = jnp.where(grow < 2 * nh, cf, cb)


def _gate_prep(g2d, bias, nh, rows):
    m = g2d.shape[0]
    nc = rows // CHUNK
    out = jax.ShapeDtypeStruct((m // CHUNK, 4 * nh, CHUNK), F32)
    ospec = pl.BlockSpec((nc, 4 * nh, CHUNK), lambda i: (i, 0, 0))
    return pl.pallas_call(
        functools.partial(_gate_kernel, nh=nh),
        out_shape=(out, out, out),
        grid=(m // rows,),
        in_specs=[pl.BlockSpec((rows, LANES), lambda i: (i, 0)),
                  pl.BlockSpec((1, LANES), lambda i: (0, 0))],
        out_specs=(ospec, ospec, ospec),
        compiler_params=_cparams(("arbitrary",)),
        name="gate_prep",
    )(g2d, bias)


def _mlstm_kernel(qx_ref, kx_ref, vx_ref, ox_ref, zx_ref, qc_ref, kc_ref, vc_ref, oc_ref, zc_ref,
                  btx_ref, rtx_ref, ctx_ref, btc_ref, rtc_ref, ctc_ref, gh_ref, yx_ref, yc_ref,
                  s_s, kv_s, sc_s, cp_s, mp_s, *, nh, group):
    h_idx = pl.program_id(1)
    row = lax.broadcasted_iota(jnp.int32, (CHUNK, CHUNK), 0)
    col = lax.broadcasted_iota(jnp.int32, (CHUNK, CHUNK), 1)
    masks = (row >= col, row <= col)
    ones_blk = jnp.ones((CHUNK, LANES), BF16)
    eye = row == col
    ncx = qx_ref.shape[0] // CHUNK
    ncc = qc_ref.shape[0] // CHUNK
    nct = ncc + ncx
    g_rows = (nh + h_idx, 3 * nh + h_idx)
    last = (CHUNK - 1, 0)

    trips = ncx // group
    ctx_per_trip = ncc // trips

    def both_segments(chunk_fn, latent_refs, context_refs):
        def body(i, _):
            for u in range(group):
                chunk_fn(i * group + u, ncc, *latent_refs)
            for u in range(ctx_per_trip):
                chunk_fn(i * ctx_per_trip + u, 0, *context_refs)
            return 0
        lax.fori_loop(0, trips, body, 0)

    def chunk_a(c, c0, q_ref, k_ref, v_ref, bt_ref, rt_ref, ct_ref):
        r0 = pl.multiple_of(c * CHUNK, CHUNK)
        rs = pl.multiple_of((c0 + c) * CHUNK, CHUNK)
        kt = k_ref[pl.ds(r0, CHUNK), :].astype(F32).T
        s_s[pl.ds(rs, CHUNK), :] = jnp.dot(q_ref[pl.ds(r0, CHUNK), :], kt.astype(BF16),
                                           preferred_element_type=F32)
        vaug = jnp.concatenate([v_ref[pl.ds(r0, CHUNK), :], ones_blk], axis=1)
        for d in range(2):
            b_last = bt_ref[c, pl.ds(g_rows[d], 1), last[d]:last[d] + 1]
            mg = b_last + ct_ref[c, pl.ds(g_rows[d], 1), last[d]:last[d] + 1]
            gl = b_last + rt_ref[c, pl.ds(g_rows[d], 1), :]
            kws = (kt * jnp.exp2(gl - mg)).astype(BF16)
            kv_s[d, c0 + c] = jnp.dot(kws, vaug, preferred_element_type=F32)
            sc_s[d, c0 + c, 0:1, :] = jnp.broadcast_to(b_last, (1, LANES))
            sc_s[d, c0 + c, 1:2, :] = jnp.broadcast_to(mg, (1, LANES))

    both_segments(chunk_a, (qx_ref, kx_ref, vx_ref, btx_ref, rtx_ref, ctx_ref),
                  (qc_ref, kc_ref, vc_ref, btc_ref, rtc_ref, ctc_ref))

    def scan(d):
        def body(j, carry):
            m_prev, c_prev = carry
            g = j if d == 0 else jnp.where(j < ncc, ncc - 1 - j, nct - 1 - j + ncc)
            cp_s[d, g] = c_prev.astype(BF16)
            mp_s[d, g] = jnp.broadcast_to(m_prev, (8, LANES))
            b_last = sc_s[d, g, 0:1, :]
            mg = sc_s[d, g, 1:2, :]
            m_new = jnp.maximum(b_last + m_prev, mg)
            a = jnp.exp2(b_last + m_prev - m_new)
            w = jnp.exp2(mg - m_new)
            c_new = (jnp.concatenate([a, a], axis=1) * c_prev
                     + jnp.concatenate([w, w], axis=1) * kv_s[d, g])
            return (m_new, c_new)
        lax.fori_loop(0, nct, body, (jnp.full((1, LANES), NEG, F32),
                                     jnp.zeros((LANES, 2 * LANES), F32)), unroll=2)

    scan(0)
    scan(1)

    def chunk_b(c, c0, q_ref, v_ref, o_ref, z_ref, bt_ref, rt_ref, ct_ref, y_ref):
        r0 = pl.multiple_of(c * CHUNK, CHUNK)
        rs = pl.multiple_of((c0 + c) * CHUNK, CHUNK)
        q = q_ref[pl.ds(r0, CHUNK), :].astype(F32)
        s = s_s[pl.ds(rs, CHUNK), :]
        vaug = jnp.concatenate([v_ref[pl.ds(r0, CHUNK), :], ones_blk], axis=1)
        h = None
        for d in range(2):
            c_row = ct_ref[c, pl.ds(g_rows[d], 1), :]
            r_row = rt_ref[c, pl.ds(g_rows[d], 1), :]
            m_prev = mp_s[d, c0 + c, 0:1, :]
            floor_row = jnp.minimum(
                jnp.exp2(-(bt_ref[c, pl.ds(g_rows[d], 1), :] + jnp.maximum(c_row, m_prev))), BF16_MAX)
            floor_bc = jnp.dot(jnp.where(eye, floor_row, 0.0).astype(BF16), ones_blk,
                               preferred_element_type=F32)
            cm_bc = jnp.broadcast_to(c_row, (CHUNK, CHUNK)).T
            mm = jnp.maximum(cm_bc, m_prev)
            qw = (q * jnp.exp2(m_prev - mm)).astype(BF16)
            pm = (jnp.exp2(jnp.where(masks[d], r_row - mm, NEG)) * s).astype(BF16)
            nd = jnp.dot(jnp.concatenate([qw, pm], axis=1),
                         jnp.concatenate([cp_s[d, c0 + c], vaug], axis=0),
                         preferred_element_type=F32)
            hd = nd[:, 0:LANES] / jnp.maximum(jnp.abs(nd[:, LANES:]), floor_bc)
            h = hd if h is None else h + hd
        mu = jnp.mean(h, axis=1, keepdims=True)
        hc = h - mu
        hn = hc * lax.rsqrt(jnp.mean(hc * hc, axis=1, keepdims=True) + EPS) * gh_ref[...]
        z = z_ref[pl.ds(r0, CHUNK), :].astype(F32)
        o = o_ref[pl.ds(r0, CHUNK), :].astype(F32)
        y = hn * (z / ((1.0 + jnp.exp(-o)) * (1.0 + jnp.exp(-z))))
        y_ref[pl.ds(r0, CHUNK), :] = y.astype(BF16)

    both_segments(chunk_b, (qx_ref, vx_ref, ox_ref, zx_ref, btx_ref, rtx_ref, ctx_ref, yx_ref),
                  (qc_ref, vc_ref, oc_ref, zc_ref, btc_ref, rtc_ref, ctc_ref, yc_ref))


def _mlstm(px, pc, gates_x, gates_c, g_head, *, bsz, seq, ctx_len, nh, w_conv, w_ml):
    dh = LANES
    cb = 3 * w_conv // dh
    hb = w_ml // dh

    def colspec(rows, k):
        return pl.BlockSpec((rows, dh), lambda b, h: (b, cb + k * hb + h))

    def gspec(rows):
        return pl.BlockSpec((rows // CHUNK, 4 * nh, CHUNK), lambda b, h: (b, 0, 0))

    tot = seq + ctx_len
    nct = tot // CHUNK
    group = min(8, seq // CHUNK)
    assert (seq // CHUNK) % group == 0 and (ctx_len // CHUNK) % (seq // CHUNK // group) == 0
    return pl.pallas_call(
        functools.partial(_mlstm_kernel, nh=nh, group=group),
        out_shape=(jax.ShapeDtypeStruct((bsz * seq, w_ml), BF16),
                   jax.ShapeDtypeStruct((bsz * ctx_len, w_ml), BF16)),
        grid=(bsz, nh),
        in_specs=[colspec(seq, 0), colspec(seq, 1), colspec(seq, 2), colspec(seq, 3), colspec(seq, 4),
                  colspec(ctx_len, 0), colspec(ctx_len, 1), colspec(ctx_len, 2), colspec(ctx_len, 3),
                  colspec(ctx_len, 4),
                  gspec(seq), gspec(seq), gspec(seq), gspec(ctx_len), gspec(ctx_len), gspec(ctx_len),
                  pl.BlockSpec((1, dh), lambda b, h: (0, h))],
        out_specs=(pl.BlockSpec((seq, dh), lambda b, h: (b, h)),
                   pl.BlockSpec((ctx_len, dh), lambda b, h: (b, h))),
        scratch_shapes=[pltpu.VMEM((tot, CHUNK), F32),
                        pltpu.VMEM((2, nct, dh, 2 * dh), F32),
                        pltpu.VMEM((2, nct, 8, LANES), F32),
                        pltpu.VMEM((2, nct, dh, 2 * dh), BF16),
                        pltpu.VMEM((2, nct, 8, LANES), F32)],
        compiler_params=_cparams(("arbitrary", "arbitrary")),
        name="mlstm",
    )(px, px, px, px, px, pc, pc, pc, pc, pc, *gates_x, *gates_c, g_head)


def _glu(a_ref, g_ref, r0, rows):
    a = a_ref[r0:r0 + rows, :].astype(F32)
    g = g_ref[r0:r0 + rows, :].astype(F32)
    return a * (0.5 * jnp.tanh(0.5 * g) + 0.5)


def _segconv_kernel(a_ref, g_ref, w_ref, b_ref, o_ref, pad_ref, *, seg, sub):
    rows, cw = a_ref.shape
    stride = seg + 2 * SEG_PAD
    total = rows // seg * stride
    zeros = jnp.zeros((SEG_PAD, cw), F32)
    for s in range(rows // seg):
        pad_ref[0, s * stride:s * stride + SEG_PAD, :] = zeros
        pad_ref[0, s * stride + SEG_PAD:s * stride + SEG_PAD + seg, :] = _glu(a_ref, g_ref, s * seg, seg)
        pad_ref[0, s * stride + SEG_PAD + seg:(s + 1) * stride, :] = zeros
    pad_ref[0, total:total + SUBLANES, :] = jnp.zeros((SUBLANES, cw), F32)
    for r in range(1, SUBLANES):
        pad_ref[r, 0:total, :] = pad_ref[0, r:r + total, :]
    for s in range(rows // seg):
        for t0 in range(0, seg, sub):
            start = s * stride + SEG_PAD + t0 - CONV_PAD
            acc = jnp.zeros((sub, cw), F32) + b_ref[...]
            for k in range(CONV_K):
                r = (start + k) % SUBLANES
                acc = acc + pad_ref[r, start + k - r:start + k - r + sub, :] * w_ref[k:k + 1, :]
            o_ref[s * seg + t0:s * seg + t0 + sub, :] = acc


def _segconv(p, w_dw, b_dw, *, seg, rows, cw, ncol, a_col0, g_col0, w_col0):
    m = p.shape[0]
    stride = seg + 2 * SEG_PAD
    return pl.pallas_call(
        functools.partial(_segconv_kernel, seg=seg, sub=min(seg, 64)),
        out_shape=jax.ShapeDtypeStruct((m, ncol * cw), F32),
        grid=(m // rows, ncol),
        in_specs=[pl.BlockSpec((rows, cw), lambda i, j: (i, a_col0 + j)),
                  pl.BlockSpec((rows, cw), lambda i, j: (i, g_col0 + j)),
                  pl.BlockSpec((CONV_K, cw), lambda i, j: (0, w_col0 + j)),
                  pl.BlockSpec((1, cw), lambda i, j: (0, w_col0 + j))],
        out_specs=pl.BlockSpec((rows, cw), lambda i, j: (i, j)),
        scratch_shapes=[pltpu.VMEM((SUBLANES, rows // seg * stride + SUBLANES, cw), F32)],
        compiler_params=_cparams(("arbitrary", "arbitrary")),
        name="segconv",
    )(p, p, w_dw, b_dw)


def _rowconv_kernel(a_ref, g_ref, w_ref, b_ref, *rest, nrows, with_ada):
    if with_ada:
        cond_ref, wada_ref, bada_ref, o_ref, ada_ref, u_ref = rest
        ada_ref[...] = _ada_tile(cond_ref, wada_ref, bada_ref)
    else:
        o_ref, u_ref = rest
    cw = a_ref.shape[1]
    for r in range(nrows):
        u_ref[r * GRID_W:(r + 1) * GRID_W, :] = _glu(a_ref, g_ref, r * GRID_W, GRID_W)
    for r in range(nrows):
        acc = jnp.zeros((GRID_W, cw), F32) + b_ref[...]
        for k in range(CONV_K):
            src = r + k - CONV_PAD
            if 0 <= src < nrows:
                acc = acc + u_ref[src * GRID_W:(src + 1) * GRID_W, :] * w_ref[k:k + 1, :]
        o_ref[r * GRID_W:(r + 1) * GRID_W, :] = acc


def _rowconv(p, w_dw, b_dw, *, seq, cw, ncol, a_col0, g_col0, w_col0, ada_next=None):
    m = p.shape[0]
    nrows = seq // GRID_W
    steps = (m // seq) * ncol
    in_specs = [pl.BlockSpec((seq, cw), lambda i, j: (i, a_col0 + j)),
                pl.BlockSpec((seq, cw), lambda i, j: (i, g_col0 + j)),
                pl.BlockSpec((CONV_K, cw), lambda i, j: (0, w_col0 + j)),
                pl.BlockSpec((1, cw), lambda i, j: (0, w_col0 + j))]
    args = [p, p, w_dw, b_dw]
    out_shape = [jax.ShapeDtypeStruct((m, ncol * cw), F32)]
    out_specs = [pl.BlockSpec((seq, cw), lambda i, j: (i, j))]
    if ada_next is not None:
        cond, w_ada, b_ada3, layer = ada_next
        d, n = w_ada.shape[1:]
        ta = n // steps
        assert ta * steps == n and ta % LANES == 0
        in_specs += [pl.BlockSpec((8, d), lambda i, j: (0, 0)),
                     pl.BlockSpec((None, d, ta), lambda i, j: (layer, 0, i * ncol + j)),
                     pl.BlockSpec((None, 1, ta), lambda i, j: (layer, 0, i * ncol + j))]
        args += [cond, w_ada, b_ada3]
        out_shape.append(jax.ShapeDtypeStruct((8, n), F32))
        out_specs.append(pl.BlockSpec((8, ta), lambda i, j: (0, i * ncol + j)))
    return pl.pallas_call(
        functools.partial(_rowconv_kernel, nrows=nrows, with_ada=ada_next is not None),
        out_shape=tuple(out_shape),
        grid=(m // seq, ncol),
        in_specs=in_specs,
        out_specs=tuple(out_specs),
        scratch_shapes=[pltpu.VMEM((seq, cw), F32)],
        compiler_params=_cparams(("arbitrary", "arbitrary")),
        name="rowconv",
    )(*args)


def _out_kernel(cva_ref, cvb_ref, z_ref, ym_ref, x_ref, gt_ref, lng_ref, lnb_ref, gpost_ref,
                wpw_ref, wout_ref, o_ref):
    w_conv = wpw_ref.shape[0]
    u = jnp.concatenate([cva_ref[...], cvb_ref[...]], axis=1)
    mu = jnp.mean(u, axis=-1, keepdims=True)
    uc = u - mu
    r = uc * lax.rsqrt(jnp.mean(uc * uc, axis=-1, keepdims=True) + EPS) * lng_ref[...] + lnb_ref[...]
    t = jnp.dot((r * _sigmoid(r)).astype(BF16), wpw_ref[...], preferred_element_type=F32)
    z = z_ref[...].astype(F32)
    yc = (t * (z * _sigmoid(z))).astype(BF16)
    out = (jnp.dot(yc, wout_ref[0:w_conv, :], preferred_element_type=F32)
           + jnp.dot(ym_ref[...], wout_ref[w_conv:, :], preferred_element_type=F32))
    ms = jnp.mean(out * out, axis=-1, keepdims=True)
    o_ref[...] = x_ref[...] + (out * lax.rsqrt(ms + EPS)) * (gt_ref[0] * gpost_ref[...])


def _out_proj(cva, cvb, cvb_col, p, ym, x2d, ada3, row_of_tile, ln_g, ln_b, g_post, w_pw2, w_out,
              *, tm, w_conv):
    m, d = x2d.shape
    half = w_conv // 2
    w_mix = w_out.shape[0]
    once = pl.Buffered(1)
    return pl.pallas_call(
        _out_kernel,
        out_shape=jax.ShapeDtypeStruct((m, d), F32),
        grid=(m // tm,),
        in_specs=[pl.BlockSpec((tm, half), lambda i: (i, 0)),
                  pl.BlockSpec((tm, half), lambda i: (i, cvb_col)),
                  pl.BlockSpec((tm, w_conv), lambda i: (i, 2)),
                  pl.BlockSpec((tm, w_mix - w_conv), lambda i: (i, 0)),
                  pl.BlockSpec((tm, d), lambda i: (i, 0)),
                  pl.BlockSpec((1, 1, d), lambda i: (row_of_tile(i) * 3 + 2, 0, 0)),
                  pl.BlockSpec((1, w_conv), lambda i: (0, 0)),
                  pl.BlockSpec((1, w_conv), lambda i: (0, 0)),
                  pl.BlockSpec((1, d), lambda i: (0, 0)),
                  pl.BlockSpec((w_conv, w_conv), lambda i: (0, 0), pipeline_mode=once),
                  pl.BlockSpec((w_mix, d), lambda i: (0, 0), pipeline_mode=once)],
        out_specs=pl.BlockSpec((tm, d), lambda i: (i, 0)),
        compiler_params=_cparams(("arbitrary",)),
        name="out_proj",
    )(cva, cvb, p, ym, x2d, ada3, ln_g, ln_b, g_post, w_pw2, w_out)


def kernel(x, c, ctx, c_ctx, w_ada, b_ada, g_pre, g_post, w_in, b_gate, w_dw, b_dw, ln_g, ln_b, w_pw2,
           g_head, w_out):
    bsz, seq, d = x.shape
    ctx_len = ctx.shape[1]
    depth = w_ada.shape[0]
    w_conv = w_dw.shape[-1]
    w_ml = g_head.shape[-1]
    nh = b_gate.shape[-1] // 4
    n_main = 3 * w_conv + 5 * w_ml
    half = w_conv // 2
    assert w_ml == nh * LANES and 4 * nh <= LANES and w_conv == w_ml
    assert seq % CHUNK == 0 and ctx_len % CHUNK == 0 and seq % GRID_W == 0 and half % LANES == 0

    ctx_row = bsz
    cond = jnp.concatenate([c, c_ctx[None, :], jnp.zeros((8 - bsz - 1, d), F32)], axis=0)
    ada3 = _ada_layers(cond, w_ada, b_ada, 1).reshape(8 * 3, 1, d)
    b_ada3 = b_ada.reshape(depth, 1, 3 * d)

    w_in_t = jnp.swapaxes(w_in, 1, 2)
    bias_g = jnp.pad(b_gate, ((0, 0), (0, LANES - 4 * nh)))

    tm_x = min(1024, seq)
    tm_c = min(1024, bsz * ctx_len)
    tn = min(1024, w_conv)
    tm_o = min(512, seq)
    tm_oc = min(512, bsz * ctx_len)
    cw = min(256, half)

    xs = x.reshape(bsz * seq, d)
    cs = ctx.reshape(bsz * ctx_len, d)
    for l in range(depth):
        last = l == depth - 1
        row_x = lambda i, t=seq // tm_x: i // t
        row_c = lambda i: ctx_row + 0 * i
        pc, gc, w_main_t, w_pw2_b, w_out_b = _in_proj(
            cs, ada3, row_c, g_pre[l][None], None, w_in_t, tm=tm_c, tn=tn // 2, w_conv=w_conv,
            w_ml=w_ml, nh=nh, layer=l, cast_also=(w_pw2, w_out))
        px, gx = _in_proj(xs, ada3, row_x, g_pre[l][None], w_main_t, w_in_t,
                          tm=tm_x, tn=2 * tn, w_conv=w_conv, w_ml=w_ml, nh=nh, layer=l)
        gates_x = _gate_prep(gx, bias_g[l][None], nh, min(1024, seq))
        gates_c = _gate_prep(gc, bias_g[l][None], nh, min(1024, ctx_len))
        ymx, ymc = _mlstm(px, pc, gates_x, gates_c, g_head[l][None], bsz=bsz, seq=seq,
                          ctx_len=ctx_len, nh=nh, w_conv=w_conv, w_ml=w_ml)
        ncol = half // cw
        g0 = w_conv // cw
        cv_w = _segconv(px, w_dw[l], b_dw[l][None], seg=GRID_W, rows=min(512, seq), cw=cw, ncol=ncol,
                        a_col0=0, g_col0=g0, w_col0=0)
        cv_h = _rowconv(px, w_dw[l], b_dw[l][None], seq=seq, cw=cw, ncol=ncol, a_col0=ncol,
                        g_col0=g0 + ncol, w_col0=ncol,
                        ada_next=None if last else (cond, w_ada, b_ada3, l + 1))
        if not last:
            cv_h, ada_next = cv_h
        else:
            cv_h, = cv_h
        row_xo = lambda i, t=seq // tm_o: i // t
        xs_new = _out_proj(cv_w, cv_h, 0, px, ymx, xs, ada3, row_xo, ln_g[l][None], ln_b[l][None],
                           g_post[l][None], w_pw2_b, w_out_b, tm=tm_o, w_conv=w_conv)
        if not last:
            cv_c = _segconv(pc, w_dw[l], b_dw[l][None], seg=ctx_len, rows=ctx_len, cw=cw,
                            ncol=2 * ncol, a_col0=0, g_col0=g0, w_col0=0)
            cs = _out_proj(cv_c, cv_c, 1, pc, ymc, cs, ada3, row_c, ln_g[l][None], ln_b[l][None],
                           g_post[l][None], w_pw2_b, w_out_b, tm=tm_oc, w_conv=w_conv)
            ada3 = ada_next.reshape(8 * 3, 1, d)
        xs = xs_new
    return xs.reshape(bsz, seq, d)
```

```python
import functools

import jax
import jax.numpy as jnp
from jax import lax
from jax.experimental import pallas as pl
from jax.experimental.pallas import tpu as pltpu

F32 = jnp.float32
BF16 = jnp.bfloat16

GRID_W = 64
CONV_K = 31
CONV_PAD = CONV_K // 2
EPS = 1e-6
NEG = -1e30
LOG2E = 1.4426950408889634
BF16_MAX = (2.0 - 2.0 ** -7) * 2.0 ** 127
LANES = 128
SUBLANES = 8
CHUNK = 128
SEG_PAD = 16
VMEM_LIMIT = 56 * 1024 * 1024


def _cparams(sem):
    return pltpu.CompilerParams(dimension_semantics=sem, vmem_limit_bytes=VMEM_LIMIT)


def _sigmoid(x):
    return 1.0 / (1.0 + jnp.exp(-x))


def _ada_tile(c_ref, w_ref, b_ref):
    c = c_ref[...]
    s = (c * _sigmoid(c)).astype(BF16)
    return jnp.dot(s, w_ref[...].astype(BF16), preferred_element_type=F32) + b_ref[...]


def _ada_kernel(c_ref, w_ref, b_ref, o_ref):
    o_ref[0] = _ada_tile(c_ref, w_ref.at[0], b_ref.at[0])


def _ada_layers(cond, w_ada, b_ada, layers):
    depth, d, n = w_ada.shape
    tn = 1024 if n % 1024 == 0 else n // 3
    return pl.pallas_call(
        _ada_kernel,
        out_shape=jax.ShapeDtypeStruct((layers, 8, n), F32),
        grid=(layers, n // tn),
        in_specs=[pl.BlockSpec((8, d), lambda l, j: (0, 0)),
                  pl.BlockSpec((1, d, tn), lambda l, j: (l, 0, j)),
                  pl.BlockSpec((1, 1, tn), lambda l, j: (l, 0, j))],
        out_specs=pl.BlockSpec((1, 8, tn), lambda l, j: (l, 0, j)),
        compiler_params=_cparams(("arbitrary", "arbitrary")),
        name="ada",
    )(cond, w_ada, b_ada.reshape(depth, 1, n))


_NT = (((1,), (1,)), ((), ()))


def _in_kernel(x_ref, sh_ref, sc_ref, gpre_ref, w_ref, wg_ref, *rest,
               tn, q_lo, q_hi, q_scale, sub, emit_w):
    if emit_w:
        wpw32_ref, wout32_ref, p_ref, g_ref, wb_ref, wpwb_ref, woutb_ref, hx_ref = rest
        wpwb_ref[...] = wpw32_ref[...].astype(BF16)
        woutb_ref[...] = wout32_ref[...].astype(BF16)
    else:
        p_ref, g_ref, hx_ref = rest
    j = pl.program_id(1)

    @pl.when(j == 0)
    def _():
        tm, d = x_ref.shape
        gain = gpre_ref[...] * (1.0 + sc_ref[0])
        for r in range(0, tm, sub):
            x = x_ref[r:r + sub, :]
            ms = jnp.mean(x * x, axis=-1, keepdims=True)
            h = (x * lax.rsqrt(ms + EPS)) * gain + sh_ref[0]
            hx_ref[r:r + sub, :] = h.astype(BF16)
        wg = wg_ref[...].astype(BF16)
        wg = jnp.concatenate([wg, jnp.zeros((LANES - wg.shape[0], d), BF16)], axis=0)
        g_ref[...] = lax.dot_general(hx_ref[...], wg, _NT, preferred_element_type=F32)

    w = w_ref[...]
    if emit_w:
        w = w.astype(BF16)
        wb_ref[...] = w
    acc = lax.dot_general(hx_ref[...], w, _NT, preferred_element_type=F32)
    col = j * tn + lax.broadcasted_iota(jnp.int32, (1, tn), 1)
    scale = jnp.where((col >= q_lo) & (col < q_hi), q_scale, 1.0).astype(F32)
    p_ref[...] = (acc * scale).astype(BF16)


def _in_proj(x2d, ada3, row_of_tile, g_pre, w_t, w_in_t, *, tm, tn, w_conv, w_ml, nh, layer,
             cast_also=None):
    m, d = x2d.shape
    n = 3 * w_conv + 5 * w_ml
    q_lo = 3 * w_conv
    steps = n // tn
    emit_w = cast_also is not None
    kern = functools.partial(_in_kernel, tn=tn, q_lo=q_lo, q_hi=q_lo + w_ml,
                             q_scale=float(LANES) ** -0.5, sub=min(tm, 256), emit_w=emit_w)
    in_specs = [pl.BlockSpec((tm, d), lambda i, j: (i, 0)),
                pl.BlockSpec((1, 1, d), lambda i, j: (row_of_tile(i) * 3 + 0, 0, 0)),
                pl.BlockSpec((1, 1, d), lambda i, j: (row_of_tile(i) * 3 + 1, 0, 0)),
                pl.BlockSpec((1, d), lambda i, j: (0, 0)),
                pl.BlockSpec((tn, d), lambda i, j: (j, 0)),
                pl.BlockSpec((None, 4 * nh, d), lambda i, j: (layer, n // (4 * nh), 0))]
    args = [x2d, ada3, ada3, g_pre, w_t, w_in_t]
    out_shape = [jax.ShapeDtypeStruct((m, n), BF16), jax.ShapeDtypeStruct((m, LANES), F32)]
    out_specs = [pl.BlockSpec((tm, tn), lambda i, j: (i, j)),
                 pl.BlockSpec((tm, LANES), lambda i, j: (i, 0))]
    if emit_w:
        assert m == tm
        in_specs[4] = pl.BlockSpec((None, tn, d), lambda i, j: (layer, j, 0))
        args[4] = w_in_t
        out_shape.append(jax.ShapeDtypeStruct((n, d), BF16))
        out_specs.append(pl.BlockSpec((tn, d), lambda i, j: (j, 0)))
        for w32 in cast_also:
            rows, cols = w32.shape[1] // steps, w32.shape[2]
            in_specs.append(pl.BlockSpec((None, rows, cols), lambda i, j: (layer, j, 0)))
            args.append(w32)
            out_shape.append(jax.ShapeDtypeStruct(w32.shape[1:], BF16))
            out_specs.append(pl.BlockSpec((rows, cols), lambda i, j: (j, 0)))
    return pl.pallas_call(
        kern,
        out_shape=tuple(out_shape),
        grid=(m // tm, steps),
        in_specs=in_specs,
        out_specs=tuple(out_specs),
        scratch_shapes=[pltpu.VMEM((tm, d), BF16)],
        compiler_params=_cparams(("arbitrary", "arbitrary")),
        name="in_proj",
    )(*args)


def _gate_kernel(g_ref, bias_ref, bt_ref, rt_ref, ct_ref, *, nh):
    row = lax.broadcasted_iota(jnp.int32, (CHUNK, CHUNK), 0)
    col = lax.broadcasted_iota(jnp.int32, (CHUNK, CHUNK), 1)
    tril = jnp.where(row >= col, 1.0, 0.0).astype(BF16)
    is_ff = (col >= nh) & (col < 2 * nh)
    is_fb = (col >= 3 * nh) & (col < 4 * nh)
    grow = lax.broadcasted_iota(jnp.int32, (4 * nh, CHUNK), 0)
    tok = lax.broadcasted_iota(jnp.int32, (4 * nh, CHUNK), 1)

    def csum(tri, parts):
        out = jnp.dot(tri, parts[0], preferred_element_type=F32)
        for p in parts[1:]:
            out = out + jnp.dot(tri, p, preferred_element_type=F32)
        return out

    for c in range(g_ref.shape[0] // CHUNK):
        g = g_ref[c * CHUNK:(c + 1) * CHUNK, :] + bias_ref[...]
        lf = jnp.minimum(g, 0.0) - jnp.log(1.0 + jnp.exp(-jnp.abs(g)))
        hi = lf.astype(BF16)
        r1 = lf - hi.astype(F32)
        mid = r1.astype(BF16)
        lo = (r1 - mid.astype(F32)).astype(BF16)
        pre = csum(tril, (hi, mid, lo))
        suf = pre[CHUNK - 1:CHUNK, :] - pre + lf
        bmat = jnp.where(is_ff, pre, jnp.where(is_fb, suf, 0.0))
        li = pltpu.roll(g, nh, axis=1)
        rmat = li - bmat
        bt_ref[c] = bmat.T[0:4 * nh, :] * LOG2E
        rt = rmat.T[0:4 * nh, :] * LOG2E
        rt_ref[c] = rt
        cf = rt
        cb = rt
        sh = 1
        while sh < CHUNK:
            cf = jnp.maximum(cf, jnp.where(tok >= sh, pltpu.roll(cf, sh, axis=1), NEG))
            cb = jnp.maximum(cb, jnp.where(tok < CHUNK - sh, pltpu.roll(cb, CHUNK - sh, axis=1), NEG))
            sh *= 2
        ct_ref[c] = jnp.where(grow < 2 * nh, cf, cb)


def _gate_prep(g2d, bias, nh, rows):
    m = g2d.shape[0]
    nc = rows // CHUNK
    out = jax.ShapeDtypeStruct((m // CHUNK, 4 * nh, CHUNK), F32)
    ospec = pl.BlockSpec((nc, 4 * nh, CHUNK), lambda i: (i, 0, 0))
    return pl.pallas_call(
        functools.partial(_gate_kernel, nh=nh),
        out_shape=(out, out, out),
        grid=(m // rows,),
        in_specs=[pl.BlockSpec((rows, LANES), lambda i: (i, 0)),
                  pl.BlockSpec((1, LANES), lambda i: (0, 0))],
        out_specs=(ospec, ospec, ospec),
        compiler_params=_cparams(("arbitrary",)),
        name="gate_prep",
    )(g2d, bias)


def _mlstm_kernel(qx_ref, kx_ref, vx_ref, ox_ref, zx_ref, qc_ref, kc_ref, vc_ref, oc_ref, zc_ref,
                  btx_ref, rtx_ref, ctx_ref, btc_ref, rtc_ref, ctc_ref, gh_ref, yx_ref, yc_ref,
                  s_s, kv_s, sc_s, cp_s, mp_s, *, nh, group):
    h_idx = pl.program_id(1)
    row = lax.broadcasted_iota(jnp.int32, (CHUNK, CHUNK), 0)
    col = lax.broadcasted_iota(jnp.int32, (CHUNK, CHUNK), 1)
    masks = (row >= col, row <= col)
    ones_blk = jnp.ones((CHUNK, LANES), BF16)
    eye = row == col
    ncx = qx_ref.shape[0] // CHUNK
    ncc = qc_ref.shape[0] // CHUNK
    nct = ncc + ncx
    g_rows = (nh + h_idx, 3 * nh + h_idx)
    last = (CHUNK - 1, 0)

    trips = ncx // group
    ctx_per_trip = ncc // trips

    def both_segments(chunk_fn, latent_refs, context_refs):
        def body(i, _):
            for u in range(group):
                chunk_fn(i * group + u, ncc, *latent_refs)
            for u in range(ctx_per_trip):
                chunk_fn(i * ctx_per_trip + u, 0, *context_refs)
            return 0
        lax.fori_loop(0, trips, body, 0)

    def chunk_a(c, c0, q_ref, k_ref, v_ref, bt_ref, rt_ref, ct_ref):
        r0 = pl.multiple_of(c * CHUNK, CHUNK)
        rs = pl.multiple_of((c0 + c) * CHUNK, CHUNK)
        kt = k_ref[pl.ds(r0, CHUNK), :].astype(F32).T
        s_s[pl.ds(rs, CHUNK), :] = jnp.dot(q_ref[pl.ds(r0, CHUNK), :], kt.astype(BF16),
                                           preferred_element_type=F32)
        vaug = jnp.concatenate([v_ref[pl.ds(r0, CHUNK), :], ones_blk], axis=1)
        for d in range(2):
            b_last = bt_ref[c, pl.ds(g_rows[d], 1), last[d]:last[d] + 1]
            mg = b_last + ct_ref[c, pl.ds(g_rows[d], 1), last[d]:last[d] + 1]
            gl = b_last + rt_ref[c, pl.ds(g_rows[d], 1), :]
            kws = (kt * jnp.exp2(gl - mg)).astype(BF16)
            kv_s[d, c0 + c] = jnp.dot(kws, vaug, preferred_element_type=F32)
            sc_s[d, c0 + c, 0:1, :] = jnp.broadcast_to(b_last, (1, LANES))
            sc_s[d, c0 + c, 1:2, :] = jnp.broadcast_to(mg, (1, LANES))

    both_segments(chunk_a, (qx_ref, kx_ref, vx_ref, btx_ref, rtx_ref, ctx_ref),
                  (qc_ref, kc_ref, vc_ref, btc_ref, rtc_ref, ctc_ref))

    def scan(d):
        def body(j, carry):
            m_prev, c_prev = carry
            g = j if d == 0 else jnp.where(j < ncc, ncc - 1 - j, nct - 1 - j + ncc)
            cp_s[d, g] = c_prev.astype(BF16)
            mp_s[d, g] = jnp.broadcast_to(m_prev, (8, LANES))
            b_last = sc_s[d, g, 0:1, :]
            mg = sc_s[d, g, 1:2, :]
            m_new = jnp.maximum(b_last + m_prev, mg)
            a = jnp.exp2(b_last + m_prev - m_new)
            w = jnp.exp2(mg - m_new)
            c_new = (jnp.concatenate([a, a], axis=1) * c_prev
                     + jnp.concatenate([w, w], axis=1) * kv_s[d, g])
            return (m_new, c_new)
        lax.fori_loop(0, nct, body, (jnp.full((1, LANES), NEG, F32),
                                     jnp.zeros((LANES, 2 * LANES), F32)), unroll=2)

    scan(0)
    scan(1)

    def chunk_b(c, c0, q_ref, v_ref, o_ref, z_ref, bt_ref, rt_ref, ct_ref, y_ref):
        r0 = pl.multiple_of(c * CHUNK, CHUNK)
        rs = pl.multiple_of((c0 + c) * CHUNK, CHUNK)
        q = q_ref[pl.ds(r0, CHUNK), :].astype(F32)
        s = s_s[pl.ds(rs, CHUNK), :]
        vaug = jnp.concatenate([v_ref[pl.ds(r0, CHUNK), :], ones_blk], axis=1)
        h = None
        for d in range(2):
            c_row = ct_ref[c, pl.ds(g_rows[d], 1), :]
            r_row = rt_ref[c, pl.ds(g_rows[d], 1), :]
            m_prev = mp_s[d, c0 + c, 0:1, :]
            floor_row = jnp.minimum(
                jnp.exp2(-(bt_ref[c, pl.ds(g_rows[d], 1), :] + jnp.maximum(c_row, m_prev))), BF16_MAX)
            floor_bc = jnp.dot(jnp.where(eye, floor_row, 0.0).astype(BF16), ones_blk,
                               preferred_element_type=F32)
            cm_bc = jnp.broadcast_to(c_row, (CHUNK, CHUNK)).T
            mm = jnp.maximum(cm_bc, m_prev)
            qw = (q * jnp.exp2(m_prev - mm)).astype(BF16)
            pm = (jnp.exp2(jnp.where(masks[d], r_row - mm, NEG)) * s).astype(BF16)
            nd = jnp.dot(jnp.concatenate([qw, pm], axis=1),
                         jnp.concatenate([cp_s[d, c0 + c], vaug], axis=0),
                         preferred_element_type=F32)
            hd = nd[:, 0:LANES] / jnp.maximum(jnp.abs(nd[:, LANES:]), floor_bc)
            h = hd if h is None else h + hd
        mu = jnp.mean(h, axis=1, keepdims=True)
        hc = h - mu
        hn = hc * lax.rsqrt(jnp.mean(hc * hc, axis=1, keepdims=True) + EPS) * gh_ref[...]
        z = z_ref[pl.ds(r0, CHUNK), :].astype(F32)
        o = o_ref[pl.ds(r0, CHUNK), :].astype(F32)
        y = hn * (z / ((1.0 + jnp.exp(-o)) * (1.0 + jnp.exp(-z))))
        y_ref[pl.ds(r0, CHUNK), :] = y.astype(BF16)

    both_segments(chunk_b, (qx_ref, vx_ref, ox_ref, zx_ref, btx_ref, rtx_ref, ctx_ref, yx_ref),
                  (qc_ref, vc_ref, oc_ref, zc_ref, btc_ref, rtc_ref, ctc_ref, yc_ref))


def _mlstm(px, pc, gates_x, gates_c, g_head, *, bsz, seq, ctx_len, nh, w_conv, w_ml):
    dh = LANES
    cb = 3 * w_conv // dh
    hb = w_ml // dh

    def colspec(rows, k):
        return pl.BlockSpec((rows, dh), lambda b, h: (b, cb + k * hb + h))

    def gspec(rows):
        return pl.BlockSpec((rows // CHUNK, 4 * nh, CHUNK), lambda b, h: (b, 0, 0))

    tot = seq + ctx_len
    nct = tot // CHUNK
    group = min(8, seq // CHUNK)
    assert (seq // CHUNK) % group == 0 and (ctx_len // CHUNK) % (seq // CHUNK // group) == 0
    return pl.pallas_call(
        functools.partial(_mlstm_kernel, nh=nh, group=group),
        out_shape=(jax.ShapeDtypeStruct((bsz * seq, w_ml), BF16),
                   jax.ShapeDtypeStruct((bsz * ctx_len, w_ml), BF16)),
        grid=(bsz, nh),
        in_specs=[colspec(seq, 0), colspec(seq, 1), colspec(seq, 2), colspec(seq, 3), colspec(seq, 4),
                  colspec(ctx_len, 0), colspec(ctx_len, 1), colspec(ctx_len, 2), colspec(ctx_len, 3),
                  colspec(ctx_len, 4),
                  gspec(seq), gspec(seq), gspec(seq), gspec(ctx_len), gspec(ctx_len), gspec(ctx_len),
                  pl.BlockSpec((1, dh), lambda b, h: (0, h))],
        out_specs=(pl.BlockSpec((seq, dh), lambda b, h: (b, h)),
                   pl.BlockSpec((ctx_len, dh), lambda b, h: (b, h))),
        scratch_shapes=[pltpu.VMEM((tot, CHUNK), F32),
                        pltpu.VMEM((2, nct, dh, 2 * dh), F32),
                        pltpu.VMEM((2, nct, 8, LANES), F32),
                        pltpu.VMEM((2, nct, dh, 2 * dh), BF16),
                        pltpu.VMEM((2, nct, 8, LANES), F32)],
        compiler_params=_cparams(("arbitrary", "arbitrary")),
        name="mlstm",
    )(px, px, px, px, px, pc, pc, pc, pc, pc, *gates_x, *gates_c, g_head)


def _glu(a_ref, g_ref, r0, rows):
    a = a_ref[r0:r0 + rows, :].astype(F32)
    g = g_ref[r0:r0 + rows, :].astype(F32)
    return a * (0.5 * jnp.tanh(0.5 * g) + 0.5)


def _segconv_kernel(a_ref, g_ref, w_ref, b_ref, o_ref, pad_ref, *, seg, sub):
    rows, cw = a_ref.shape
    stride = seg + 2 * SEG_PAD
    total = rows // seg * stride
    zeros = jnp.zeros((SEG_PAD, cw), F32)
    for s in range(rows // seg):
        pad_ref[0, s * stride:s * stride + SEG_PAD, :] = zeros
        pad_ref[0, s * stride + SEG_PAD:s * stride + SEG_PAD + seg, :] = _glu(a_ref, g_ref, s * seg, seg)
        pad_ref[0, s * stride + SEG_PAD + seg:(s + 1) * stride, :] = zeros
    pad_ref[0, total:total + SUBLANES, :] = jnp.zeros((SUBLANES, cw), F32)
    for r in range(1, SUBLANES):
        pad_ref[r, 0:total, :] = pad_ref[0, r:r + total, :]
    for s in range(rows // seg):
        for t0 in range(0, seg, sub):
            start = s * stride + SEG_PAD + t0 - CONV_PAD
            acc = jnp.zeros((sub, cw), F32) + b_ref[...]
            for k in range(CONV_K):
                r = (start + k) % SUBLANES
                acc = acc + pad_ref[r, start + k - r:start + k - r + sub, :] * w_ref[k:k + 1, :]
            o_ref[s * seg + t0:s * seg + t0 + sub, :] = acc


def _segconv(p, w_dw, b_dw, *, seg, rows, cw, ncol, a_col0, g_col0, w_col0):
    m = p.shape[0]
    stride = seg + 2 * SEG_PAD
    return pl.pallas_call(
        functools.partial(_segconv_kernel, seg=seg, sub=min(seg, 64)),
        out_shape=jax.ShapeDtypeStruct((m, ncol * cw), F32),
        grid=(m // rows, ncol),
        in_specs=[pl.BlockSpec((rows, cw), lambda i, j: (i, a_col0 + j)),
                  pl.BlockSpec((rows, cw), lambda i, j: (i, g_col0 + j)),
                  pl.BlockSpec((CONV_K, cw), lambda i, j: (0, w_col0 + j)),
                  pl.BlockSpec((1, cw), lambda i, j: (0, w_col0 + j))],
        out_specs=pl.BlockSpec((rows, cw), lambda i, j: (i, j)),
        scratch_shapes=[pltpu.VMEM((SUBLANES, rows // seg * stride + SUBLANES, cw), F32)],
        compiler_params=_cparams(("arbitrary", "arbitrary")),
        name="segconv",
    )(p, p, w_dw, b_dw)


def _rowconv_kernel(a_ref, g_ref, w_ref, b_ref, *rest, nrows, with_ada):
    if with_ada:
        cond_ref, wada_ref, bada_ref, o_ref, ada_ref, u_ref = rest
        ada_ref[...] = _ada_tile(cond_ref, wada_ref, bada_ref)
    else:
        o_ref, u_ref = rest
    cw = a_ref.shape[1]
    for r in range(nrows):
        u_ref[r * GRID_W:(r + 1) * GRID_W, :] = _glu(a_ref, g_ref, r * GRID_W, GRID_W)
    for r in range(nrows):
        acc = jnp.zeros((GRID_W, cw), F32) + b_ref[...]
        for k in range(CONV_K):
            src = r + k - CONV_PAD
            if 0 <= src < nrows:
                acc = acc + u_ref[src * GRID_W:(src + 1) * GRID_W, :] * w_ref[k:k + 1, :]
        o_ref[r * GRID_W:(r + 1) * GRID_W, :] = acc


def _rowconv(p, w_dw, b_dw, *, seq, cw, ncol, a_col0, g_col0, w_col0, ada_next=None):
    m = p.shape[0]
    nrows = seq // GRID_W
    steps = (m // seq) * ncol
    in_specs = [pl.BlockSpec((seq, cw), lambda i, j: (i, a_col0 + j)),
                pl.BlockSpec((seq, cw), lambda i, j: (i, g_col0 + j)),
                pl.BlockSpec((CONV_K, cw), lambda i, j: (0, w_col0 + j)),
                pl.BlockSpec((1, cw), lambda i, j: (0, w_col0 + j))]
    args = [p, p, w_dw, b_dw]
    out_shape = [jax.ShapeDtypeStruct((m, ncol * cw), F32)]
    out_specs = [pl.BlockSpec((seq, cw), lambda i, j: (i, j))]
    if ada_next is not None:
        cond, w_ada, b_ada3, layer = ada_next
        d, n = w_ada.shape[1:]
        ta = n // steps
        assert ta * steps == n and ta % LANES == 0
        in_specs += [pl.BlockSpec((8, d), lambda i, j: (0, 0)),
                     pl.BlockSpec((None, d, ta), lambda i, j: (layer, 0, i * ncol + j)),
                     pl.BlockSpec((None, 1, ta), lambda i, j: (layer, 0, i * ncol + j))]
        args += [cond, w_ada, b_ada3]
        out_shape.append(jax.ShapeDtypeStruct((8, n), F32))
        out_specs.append(pl.BlockSpec((8, ta), lambda i, j: (0, i * ncol + j)))
    return pl.pallas_call(
        functools.partial(_rowconv_kernel, nrows=nrows, with_ada=ada_next is not None),
        out_shape=tuple(out_shape),
        grid=(m // seq, ncol),
        in_specs=in_specs,
        out_specs=tuple(out_specs),
        scratch_shapes=[pltpu.VMEM((seq, cw), F32)],
        compiler_params=_cparams(("arbitrary", "arbitrary")),
        name="rowconv",
    )(*args)


def _out_kernel(cva_ref, cvb_ref, z_ref, ym_ref, x_ref, gt_ref, lng_ref, lnb_ref, gpost_ref,
                wpw_ref, wout_ref, o_ref):
    w_conv = wpw_ref.shape[0]
    u = jnp.concatenate([cva_ref[...], cvb_ref[...]], axis=1)
    mu = jnp.mean(u, axis=-1, keepdims=True)
    uc = u - mu
    r = uc * lax.rsqrt(jnp.mean(uc * uc, axis=-1, keepdims=True) + EPS) * lng_ref[...] + lnb_ref[...]
    t = jnp.dot((r * _sigmoid(r)).astype(BF16), wpw_ref[...], preferred_element_type=F32)
    z = z_ref[...].astype(F32)
    yc = (t * (z * _sigmoid(z))).astype(BF16)
    out = (jnp.dot(yc, wout_ref[0:w_conv, :], preferred_element_type=F32)
           + jnp.dot(ym_ref[...], wout_ref[w_conv:, :], preferred_element_type=F32))
    ms = jnp.mean(out * out, axis=-1, keepdims=True)
    o_ref[...] = x_ref[...] + (out * lax.rsqrt(ms + EPS)) * (gt_ref[0] * gpost_ref[...])


def _out_proj(cva, cvb, cvb_col, p, ym, x2d, ada3, row_of_tile, ln_g, ln_b, g_post, w_pw2, w_out,
              *, tm, w_conv):
    m, d = x2d.shape
    half = w_conv // 2
    w_mix = w_out.shape[0]
    once = pl.Buffered(1)
    return pl.pallas_call(
        _out_kernel,
        out_shape=jax.ShapeDtypeStruct((m, d), F32),
        grid=(m // tm,),
        in_specs=[pl.BlockSpec((tm, half), lambda i: (i, 0)),
                  pl.BlockSpec((tm, half), lambda i: (i, cvb_col)),
                  pl.BlockSpec((tm, w_conv), lambda i: (i, 2)),
                  pl.BlockSpec((tm, w_mix - w_conv), lambda i: (i, 0)),
                  pl.BlockSpec((tm, d), lambda i: (i, 0)),
                  pl.BlockSpec((1, 1, d), lambda i: (row_of_tile(i) * 3 + 2, 0, 0)),
                  pl.BlockSpec((1, w_conv), lambda i: (0, 0)),
                  pl.BlockSpec((1, w_conv), lambda i: (0, 0)),
                  pl.BlockSpec((1, d), lambda i: (0, 0)),
                  pl.BlockSpec((w_conv, w_conv), lambda i: (0, 0), pipeline_mode=once),
                  pl.BlockSpec((w_mix, d), lambda i: (0, 0), pipeline_mode=once)],
        out_specs=pl.BlockSpec((tm, d), lambda i: (i, 0)),
        compiler_params=_cparams(("arbitrary",)),
        name="out_proj",
    )(cva, cvb, p, ym, x2d, ada3, ln_g, ln_b, g_post, w_pw2, w_out)


def kernel(x, c, ctx, c_ctx, w_ada, b_ada, g_pre, g_post, w_in, b_gate, w_dw, b_dw, ln_g, ln_b, w_pw2,
           g_head, w_out):
    bsz, seq, d = x.shape
    ctx_len = ctx.shape[1]
    depth = w_ada.shape[0]
    w_conv = w_dw.shape[-1]
    w_ml = g_head.shape[-1]
    nh = b_gate.shape[-1] // 4
    n_main = 3 * w_conv + 5 * w_ml
    half = w_conv // 2
    assert w_ml == nh * LANES and 4 * nh <= LANES and w_conv == w_ml
    assert seq % CHUNK == 0 and ctx_len % CHUNK == 0 and seq % GRID_W == 0 and half % LANES == 0

    ctx_row = bsz
    cond = jnp.concatenate([c, c_ctx[None, :], jnp.zeros((8 - bsz - 1, d), F32)], axis=0)
    ada3 = _ada_layers(cond, w_ada, b_ada, 1).reshape(8 * 3, 1, d)
    b_ada3 = b_ada.reshape(depth, 1, 3 * d)

    w_in_t = jnp.swapaxes(w_in, 1, 2)
    bias_g = jnp.pad(b_gate, ((0, 0), (0, LANES - 4 * nh)))

    tm_x = min(1024, seq)
    tm_c = min(1024, bsz * ctx_len)
    tn = min(1024, w_conv)
    tm_o = min(512, seq)
    tm_oc = min(256, ctx_len)
    cw = min(256, half)

    xs = x.reshape(bsz * seq, d)
    cs = ctx.reshape(bsz * ctx_len, d)
    for l in range(depth):
        last = l == depth - 1
        row_x = lambda i, t=seq // tm_x: i // t
        row_c = lambda i: ctx_row + 0 * i
        pc, gc, w_main_t, w_pw2_b, w_out_b = _in_proj(
            cs, ada3, row_c, g_pre[l][None], None, w_in_t, tm=tm_c, tn=tn // 2, w_conv=w_conv,
            w_ml=w_ml, nh=nh, layer=l, cast_also=(w_pw2, w_out))
        px, gx = _in_proj(xs, ada3, row_x, g_pre[l][None], w_main_t, w_in_t,
                          tm=tm_x, tn=2 * tn, w_conv=w_conv, w_ml=w_ml, nh=nh, layer=l)
        gates_x = _gate_prep(gx, bias_g[l][None], nh, min(1024, seq))
        gates_c = _gate_prep(gc, bias_g[l][None], nh, min(1024, bsz * ctx_len))
        ymx, ymc = _mlstm(px, pc, gates_x, gates_c, g_head[l][None], bsz=bsz, seq=seq,
                          ctx_len=ctx_len, nh=nh, w_conv=w_conv, w_ml=w_ml)
        ncol = half // cw
        g0 = w_conv // cw
        cv_w = _segconv(px, w_dw[l], b_dw[l][None], seg=GRID_W, rows=min(512, seq), cw=cw, ncol=ncol,
                        a_col0=0, g_col0=g0, w_col0=0)
        cv_h = _rowconv(px, w_dw[l], b_dw[l][None], seq=seq, cw=cw, ncol=ncol, a_col0=ncol,
                        g_col0=g0 + ncol, w_col0=ncol,
                        ada_next=None if last else (cond, w_ada, b_ada3, l + 1))
        if not last:
            cv_h, ada_next = cv_h
        else:
            cv_h, = cv_h
        row_xo = lambda i, t=seq // tm_o: i // t
        xs_new = _out_proj(cv_w, cv_h, 0, px, ymx, xs, ada3, row_xo, ln_g[l][None], ln_b[l][None],
                           g_post[l][None], w_pw2_b, w_out_b, tm=tm_o, w_conv=w_conv)
        if not last:
            cv_c = _segconv(pc, w_dw[l], b_dw[l][None], seg=ctx_len, rows=ctx_len, cw=half,
                            ncol=2, a_col0=0, g_col0=2, w_col0=0)
            cs = _out_proj(cv_c, cv_c, 1, pc, ymc, cs, ada3, row_c, ln_g[l][None], ln_b[l][None],
                           g_post[l][None], w_pw2_b, w_out_b, tm=tm_oc, w_conv=w_conv)
            ada3 = ada_next.reshape(8 * 3, 1, d)
        xs = xs_new
    return xs.reshape(bsz, seq, d)
```
